```python
import math
import jax
import jax.numpy as jnp
from jax import lax
import numpy as np

D_MODEL = 1024
BATCH = 4
SEQ = 4096
DEPTH = 1
DEC_BATCH = 32
DEC_SEQ = 8
PAST_LEN = 16384
PAGE_SIZE = 128

D_FF = 2816
D_SSM = 2 * D_MODEL
SSD_HEAD_DIM = 64
SSD_HEADS = D_SSM // SSD_HEAD_DIM
SSD_GROUPS = 4
D_STATE = 128
D_CONV = 4
CONV_DIM = D_SSM + 2 * SSD_GROUPS * D_STATE
SSD_CHUNK = 128
ATT_HEAD_DIM = 64
ATT_HEADS = D_MODEL // ATT_HEAD_DIM
KV_HEADS = 4
Q_PER_KV = ATT_HEADS // KV_HEADS
ATT_DIM = ATT_HEADS * ATT_HEAD_DIM
KV_DIM = KV_HEADS * ATT_HEAD_DIM
Q_BLOCK = 128
IN_DIM = D_SSM + CONV_DIM + SSD_HEADS + ATT_DIM + 2 * KV_DIM + ATT_HEADS + 2 * D_MODEL
RMS_EPS = 1e-6

kernel_name = 'hybrid_ssd_fox_macaron_step'


def rmsnorm(x, w):
    xf = x.astype(jnp.float32)
    y = xf * lax.rsqrt(jnp.mean(xf * xf, axis=-1, keepdims=True) + RMS_EPS)
    return (y * w.astype(jnp.float32)).astype(x.dtype)


def swiglu(x, w_gate, w_up, w_down):
    return (jax.nn.silu(x @ w_gate) * (x @ w_up)) @ w_down


def split_combined(proj):
    sizes = [D_SSM, CONV_DIM, SSD_HEADS, ATT_DIM, KV_DIM, KV_DIM, ATT_HEADS, D_MODEL, D_MODEL]
    offsets = [int(o) for o in np.cumsum(sizes)[:-1]]
    return jnp.split(proj, offsets, axis=-1)


def causal_conv(xbc, conv_prev, conv_w, conv_b):
    L = xbc.shape[1]
    xa = jnp.concatenate([conv_prev.astype(xbc.dtype), xbc], axis=1)
    y = conv_b + sum(xa[:, j:j + L] * conv_w[j] for j in range(D_CONV))
    return jax.nn.silu(y), xa[:, L:]


def ssd_scan(xh, dt, A, Bm, Cm, h0):
    f32 = jnp.float32
    b, L, nh, hp = xh.shape
    r = nh // SSD_GROUPS
    Q = min(SSD_CHUNK, L)
    pad = (-L) % Q
    x = xh.astype(f32)
    Bf = Bm.astype(f32)
    Cf = Cm.astype(f32)
    if pad:
        pw = ((0, 0), (0, pad), (0, 0), (0, 0))
        x = jnp.pad(x, pw)
        Bf = jnp.pad(Bf, pw)
        Cf = jnp.pad(Cf, pw)
        dt = jnp.pad(dt, ((0, 0), (0, pad), (0, 0)))
    nc = (L + pad) // Q
    x = x.reshape(b, nc, Q, SSD_GROUPS, r, hp)
    dtc = dt.reshape(b, nc, Q, SSD_GROUPS, r)
    Bc = Bf.reshape(b, nc, Q, SSD_GROUPS, D_STATE)
    Cc = Cf.reshape(b, nc, Q, SSD_GROUPS, D_STATE)
    a = dtc * A.reshape(SSD_GROUPS, r)
    acs = jnp.cumsum(a, axis=2)
    xdt = x * dtc[..., None]
    causal = jnp.tril(jnp.ones((Q, Q), dtype=bool))
    seg = acs[:, :, :, None] - acs[:, :, None, :]
    decay = jnp.exp(jnp.where(causal[None, None, :, :, None, None], seg, -jnp.inf))
    cb = jnp.einsum('bclgn,bcsgn->bclsg', Cc, Bc)
    y_diag = jnp.einsum('bclsgr,bcsgrp->bclgrp', cb[..., None] * decay, xdt)
    states = jnp.einsum('bclgn,bclgr,bclgrp->bcgrpn', Bc, jnp.exp(acs[:, :, -1:] - acs), xdt)
    chunk_decay = jnp.exp(acs[:, :, -1])

    def step(h, inp):
        dec, st = inp
        return dec[..., None, None] * h + st, h

    h_init = h0.astype(f32).reshape(b, SSD_GROUPS, r, hp, D_STATE)
    h_fin, h_prev = lax.scan(step, h_init, (jnp.moveaxis(chunk_decay, 1, 0), jnp.moveaxis(states, 1, 0)))
    h_prev = jnp.moveaxis(h_prev, 0, 1)
    y_off = jnp.einsum('bclgn,bcgrpn,bclgr->bclgrp', Cc, h_prev, jnp.exp(acs))
    y = (y_diag + y_off).reshape(b, nc * Q, nh, hp)[:, :L]
    return y, h_fin.reshape(b, nh, hp, D_STATE).astype(h0.dtype)


def ssd_branch(z, xbc, dt_raw, conv_prev, h0, p):
    b, L, _ = z.shape
    xc, conv_new = causal_conv(xbc, conv_prev, p['conv_w'], p['conv_b'])
    xs, Bm, Cm = jnp.split(xc, [D_SSM, D_SSM + SSD_GROUPS * D_STATE], axis=-1)
    xh = xs.reshape(b, L, SSD_HEADS, SSD_HEAD_DIM)
    Bm = Bm.reshape(b, L, SSD_GROUPS, D_STATE)
    Cm = Cm.reshape(b, L, SSD_GROUPS, D_STATE)
    dt = jax.nn.softplus(dt_raw.astype(jnp.float32) + p['dt_bias'].astype(jnp.float32))
    A = -jnp.exp(p['a_log'].astype(jnp.float32))
    y, h_new = ssd_scan(xh, dt, A, Bm, Cm, h0)
    y = y + p['d_skip'].astype(jnp.float32)[:, None] * xh.astype(jnp.float32)
    yg = y.reshape(b, L, D_SSM) * jax.nn.silu(z.astype(jnp.float32))
    yg = yg.reshape(b, L, SSD_GROUPS, D_SSM // SSD_GROUPS)
    yg = yg * lax.rsqrt(jnp.mean(yg * yg, axis=-1, keepdims=True) + RMS_EPS)
    yg = yg.reshape(b, L, D_SSM) * p['ssd_norm'].astype(jnp.float32)
    return yg.astype(z.dtype), h_new, conv_new


def fox_block(q, k, v, c_q, c_k, q_pos):
    s = jnp.einsum('bqkgd,bskd->bkgqs', q, k).astype(jnp.float32) * (ATT_HEAD_DIM ** -0.5)
    bias = jnp.transpose(c_q, (0, 2, 3, 1))[..., :, None] - jnp.transpose(c_k, (0, 2, 3, 1))[..., None, :]
    mask = jnp.arange(k.shape[1])[None, :] <= q_pos[:, None]
    s = jnp.where(mask, s + bias, -jnp.inf)
    prob = jax.nn.softmax(s, axis=-1)
    return jnp.einsum('bkgqs,bskd->bqkgd', prob.astype(v.dtype), v)


def fox_attention(q, k_all, v_all, logf_all):
    b, T = q.shape[:2]
    S = k_all.shape[1]
    past = S - T
    c = jnp.cumsum(logf_all.astype(jnp.float32), axis=1).reshape(b, S, KV_HEADS, Q_PER_KV)
    c_q = c[:, past:]
    qg = q.reshape(b, T, KV_HEADS, Q_PER_KV, ATT_HEAD_DIM)
    pos = past + jnp.arange(T)
    if T > Q_BLOCK and T % Q_BLOCK == 0:
        nb = T // Q_BLOCK
        qb = jnp.moveaxis(qg.reshape(b, nb, Q_BLOCK, KV_HEADS, Q_PER_KV, ATT_HEAD_DIM), 1, 0)
        cb = jnp.moveaxis(c_q.reshape(b, nb, Q_BLOCK, KV_HEADS, Q_PER_KV), 1, 0)
        o = lax.map(lambda blk: fox_block(blk[0], k_all, v_all, blk[1], c, blk[2]),
                    (qb, cb, pos.reshape(nb, Q_BLOCK)))
        o = jnp.moveaxis(o, 0, 1)
    else:
        o = fox_block(qg, k_all, v_all, c_q, c, pos)
    return o.reshape(b, T, ATT_DIM)


def trunk_layer(x, conv_prev, h0, past, p):
    b, L, _ = x.shape
    h = x + 0.5 * swiglu(rmsnorm(x, p['ffn1_norm']), p['ffn1_w_gate'], p['ffn1_w_up'], p['ffn1_w_down'])
    xn = rmsnorm(h, p['mix_norm'])
    z, xbc, dt_raw, q, k, v, f_raw, g_ssd, g_att = split_combined(xn @ p['w_in'])
    y_ssd, h_new, conv_new = ssd_branch(z, xbc, dt_raw, conv_prev, h0, p)
    q = rmsnorm(q.reshape(b, L, ATT_HEADS, ATT_HEAD_DIM), p['q_norm'])
    k = rmsnorm(k.reshape(b, L, KV_HEADS, ATT_HEAD_DIM), p['k_norm'])
    v = v.reshape(b, L, KV_HEADS, ATT_HEAD_DIM)
    logf = jax.nn.log_sigmoid(f_raw.astype(jnp.float32) + p['b_f'].astype(jnp.float32))
    if past is None:
        k_all, v_all, logf_all = k, v, logf
    else:
        k_past, v_past, logf_past = past
        k_all = jnp.concatenate([k_past.astype(k.dtype), k], axis=1)
        v_all = jnp.concatenate([v_past.astype(v.dtype), v], axis=1)
        logf_all = jnp.concatenate([logf_past.astype(jnp.float32), logf], axis=1)
    y_att = fox_attention(q, k_all, v_all, logf_all)
    merged = (jax.nn.sigmoid(g_ssd) * (y_ssd @ p['w_ssd_proj'])
              + jax.nn.sigmoid(g_att) * (y_att @ p['w_attn_proj']))
    h = h + merged @ p['w_out']
    h = h + 0.5 * swiglu(rmsnorm(h, p['ffn2_norm']), p['ffn2_w_gate'], p['ffn2_w_up'], p['ffn2_w_down'])
    return h, (k, v, logf, h_new, conv_new)


def setup_inputs(seed: int = 0) -> dict:
    key = jax.random.key(seed)
    ks = jax.random.split(key, 40)
    f32 = jnp.float32

    def nrm(k, shape, scale):
        return scale * jax.random.normal(k, shape, f32)

    n_pages = PAST_LEN // PAGE_SIZE
    n_used = DEC_BATCH * n_pages
    n_pool = n_used + max(1, n_used // 4)
    u = jax.random.uniform(ks[10], (DEPTH, SSD_HEADS), f32)
    dt0 = jnp.exp(u * (math.log(0.1) - math.log(0.001)) + math.log(0.001))
    return {
        'x_prompt': nrm(ks[0], (BATCH, SEQ, D_MODEL), 1.0),
        'x_sample': nrm(ks[1], (DEC_BATCH, DEC_SEQ, D_MODEL), 1.0),
        'cache_k': nrm(ks[2], (DEPTH, n_pool, PAGE_SIZE, KV_HEADS, ATT_HEAD_DIM), 1.0),
        'cache_v': nrm(ks[3], (DEPTH, n_pool, PAGE_SIZE, KV_HEADS, ATT_HEAD_DIM), 1.0),
        'cache_logf': jax.nn.log_sigmoid(3.0 + nrm(ks[4], (DEPTH, n_pool, PAGE_SIZE, ATT_HEADS), 1.0)),
        'state_ssm': nrm(ks[5], (DEPTH, DEC_BATCH, SSD_HEADS, SSD_HEAD_DIM, D_STATE), 0.1),
        'state_conv': nrm(ks[6], (DEPTH, DEC_BATCH, D_CONV - 1, CONV_DIM), 1.0),
        'page_table': jax.random.permutation(ks[7], n_pool)[:n_used].reshape(DEC_BATCH, n_pages).astype(jnp.int32),
        'ffn1_norm': 1.0 + nrm(ks[8], (DEPTH, D_MODEL), 0.02),
        'ffn1_w_gate': nrm(ks[9], (DEPTH, D_MODEL, D_FF), D_MODEL ** -0.5),
        'ffn1_w_up': nrm(ks[11], (DEPTH, D_MODEL, D_FF), D_MODEL ** -0.5),
        'ffn1_w_down': nrm(ks[12], (DEPTH, D_FF, D_MODEL), D_FF ** -0.5),
        'mix_norm': 1.0 + nrm(ks[13], (DEPTH, D_MODEL), 0.02),
        'w_in': nrm(ks[14], (DEPTH, D_MODEL, IN_DIM), D_MODEL ** -0.5),
        'conv_w': nrm(ks[15], (DEPTH, D_CONV, CONV_DIM), D_CONV ** -0.5),
        'conv_b': nrm(ks[16], (DEPTH, CONV_DIM), 0.01),
        'dt_bias': dt0 + jnp.log(-jnp.expm1(-dt0)),
        'a_log': jnp.log(jax.random.uniform(ks[17], (DEPTH, SSD_HEADS), f32, minval=1.0, maxval=16.0)),
        'd_skip': 1.0 + nrm(ks[18], (DEPTH, SSD_HEADS), 0.1),
        'ssd_norm': 1.0 + nrm(ks[19], (DEPTH, D_SSM), 0.02),
        'q_norm': 1.0 + nrm(ks[20], (DEPTH, ATT_HEAD_DIM), 0.02),
        'k_norm': 1.0 + nrm(ks[21], (DEPTH, ATT_HEAD_DIM), 0.02),
        'b_f': 3.0 + nrm(ks[22], (DEPTH, ATT_HEADS), 0.5),
        'w_ssd_proj': nrm(ks[23], (DEPTH, D_SSM, D_MODEL), D_SSM ** -0.5),
        'w_attn_proj': nrm(ks[24], (DEPTH, ATT_DIM, D_MODEL), ATT_DIM ** -0.5),
        'w_out': nrm(ks[25], (DEPTH, D_MODEL, D_MODEL), D_MODEL ** -0.5),
        'ffn2_norm': 1.0 + nrm(ks[26], (DEPTH, D_MODEL), 0.02),
        'ffn2_w_gate': nrm(ks[27], (DEPTH, D_MODEL, D_FF), D_MODEL ** -0.5),
        'ffn2_w_up': nrm(ks[28], (DEPTH, D_MODEL, D_FF), D_MODEL ** -0.5),
        'ffn2_w_down': nrm(ks[29], (DEPTH, D_FF, D_MODEL), D_FF ** -0.5),
    }


def reference(x_prompt, x_sample, cache_k, cache_v, cache_logf, state_ssm, state_conv, page_table,
              ffn1_norm, ffn1_w_gate, ffn1_w_up, ffn1_w_down, mix_norm, w_in, conv_w, conv_b,
              dt_bias, a_log, d_skip, ssd_norm, q_norm, k_norm, b_f, w_ssd_proj, w_attn_proj, w_out,
              ffn2_norm, ffn2_w_gate, ffn2_w_up, ffn2_w_down):
    dec_batch, n_pages = page_table.shape
    past_len = n_pages * cache_k.shape[2]
    batch = x_prompt.shape[0]
    yp, ys = x_prompt, x_sample
    prompt_states, sample_states = [], []
    for l in range(DEPTH):
        p = {
            'ffn1_norm': ffn1_norm[l], 'ffn1_w_gate': ffn1_w_gate[l], 'ffn1_w_up': ffn1_w_up[l],
            'ffn1_w_down': ffn1_w_down[l], 'mix_norm': mix_norm[l], 'w_in': w_in[l],
            'conv_w': conv_w[l], 'conv_b': conv_b[l], 'dt_bias': dt_bias[l], 'a_log': a_log[l],
            'd_skip': d_skip[l], 'ssd_norm': ssd_norm[l], 'q_norm': q_norm[l], 'k_norm': k_norm[l],
            'b_f': b_f[l], 'w_ssd_proj': w_ssd_proj[l], 'w_attn_proj': w_attn_proj[l], 'w_out': w_out[l],
            'ffn2_norm': ffn2_norm[l], 'ffn2_w_gate': ffn2_w_gate[l], 'ffn2_w_up': ffn2_w_up[l],
            'ffn2_w_down': ffn2_w_down[l],
        }
        conv0 = jnp.zeros((batch, D_CONV - 1, CONV_DIM), x_prompt.dtype)
        h0 = jnp.zeros((batch, SSD_HEADS, SSD_HEAD_DIM, D_STATE), state_ssm.dtype)
        yp, sp = trunk_layer(yp, conv0, h0, None, p)
        k_past = cache_k[l, page_table].reshape(dec_batch, past_len, KV_HEADS, ATT_HEAD_DIM)
        v_past = cache_v[l, page_table].reshape(dec_batch, past_len, KV_HEADS, ATT_HEAD_DIM)
        logf_past = cache_logf[l, page_table].reshape(dec_batch, past_len, ATT_HEADS)
        ys, ss = trunk_layer(ys, state_conv[l], state_ssm[l], (k_past, v_past, logf_past), p)
        prompt_states.append(sp)
        sample_states.append(ss)
    k_p, v_p, logf_p, ssm_p, conv_p = (jnp.stack(t) for t in zip(*prompt_states))
    k_s, v_s, logf_s, ssm_s, conv_s = (jnp.stack(t) for t in zip(*sample_states))
    return (yp, ys, k_p, v_p, logf_p, ssm_p, conv_p, k_s, v_s, logf_s, ssm_s, conv_s)
```

```python
import functools
import numpy as np
import jax
import jax.numpy as jnp
from jax import lax
from jax.experimental import pallas as pl
from jax.experimental.pallas import tpu as pltpu

f32 = jnp.float32
bf16 = jnp.bfloat16

RMS_EPS = 1e-6
D_MODEL = 1024
D_FF = 2816
D_SSM = 2048
SSD_HEAD_DIM = 64
SSD_HEADS = 32
SSD_GROUPS = 4
HEADS_PER_GROUP = SSD_HEADS // SSD_GROUPS
D_STATE = 128
D_CONV = 4
CONV_DIM = D_SSM + 2 * SSD_GROUPS * D_STATE
SSD_CHUNK = 128
ATT_HEAD_DIM = 64
ATT_HEADS = 16
KV_HEADS = 4
Q_PER_KV = ATT_HEADS // KV_HEADS
ATT_DIM = ATT_HEADS * ATT_HEAD_DIM
KV_DIM = KV_HEADS * ATT_HEAD_DIM
NEG_BIG = -1e30

COL_Z = 0
COL_GS = 2048
COL_GA = 3072
COL_Q = 4096
COL_KV = 5120
COL_XBC = 6144
W_MAIN_COLS = 9216

VMEM_LIMIT = 56 * 1024 * 1024


def _cparams(sem):
    return pltpu.CompilerParams(dimension_semantics=sem, vmem_limit_bytes=VMEM_LIMIT)


def _split2(x):
    hi = x.astype(bf16)
    lo = (x - hi.astype(f32)).astype(bf16)
    return hi, lo


def _split3(x):
    hi = x.astype(bf16)
    r = x - hi.astype(f32)
    mid = r.astype(bf16)
    lo = (r - mid.astype(f32)).astype(bf16)
    return hi, mid, lo


def _dot(a, b):
    return jnp.dot(a, b, preferred_element_type=f32)


def _dot_nt(a, b):
    return lax.dot_general(a, b, (((1,), (1,)), ((), ())), preferred_element_type=f32)


def _dot_tn(a, b):
    return lax.dot_general(a, b, (((0,), (0,)), ((), ())), preferred_element_type=f32)


def _dot2_lhs(x, m):
    hi, lo = _split2(x)
    return _dot(hi, m) + _dot(lo, m)


def _ffn_body(x_ref, nw_ref, wg_ref, wu_ref, wd_ref, o_ref, xn_ref, acc_ref):
    j = pl.program_id(1)

    @pl.when(j == 0)
    def _():
        x = x_ref[...]
        inv = lax.rsqrt(jnp.mean(x * x, axis=-1, keepdims=True) + RMS_EPS)
        xn_ref[...] = (x * inv * nw_ref[...]).astype(bf16)
        acc_ref[...] = jnp.zeros_like(acc_ref)

    xn = xn_ref[...]
    g = _dot(xn, wg_ref[...])
    u = _dot(xn, wu_ref[...])
    a = (g * jax.nn.sigmoid(g)) * u
    acc_ref[...] += _dot(a.astype(bf16), wd_ref[...])

    @pl.when(j == pl.num_programs(1) - 1)
    def _():
        o_ref[...] = x_ref[...] + 0.5 * acc_ref[...]


def _ffn(x, nw, wg, wu, wd, tm):
    n = x.shape[0]
    tf = D_FF // 2
    return pl.pallas_call(
        _ffn_body,
        grid=(n // tm, D_FF // tf),
        in_specs=[
            pl.BlockSpec((tm, D_MODEL), lambda i, j: (i, 0)),
            pl.BlockSpec((1, D_MODEL), lambda i, j: (0, 0)),
            pl.BlockSpec((D_MODEL, tf), lambda i, j: (0, j)),
            pl.BlockSpec((D_MODEL, tf), lambda i, j: (0, j)),
            pl.BlockSpec((tf, D_MODEL), lambda i, j: (j, 0)),
        ],
        out_specs=pl.BlockSpec((tm, D_MODEL), lambda i, j: (i, 0)),
        out_shape=jax.ShapeDtypeStruct((n, D_MODEL), f32),
        scratch_shapes=[pltpu.VMEM((tm, D_MODEL), bf16), pltpu.VMEM((tm, D_MODEL), f32)],
        compiler_params=_cparams(("parallel", "arbitrary")),
        name="ffn",
    )(x, nw, wg, wu, wd)


def _nmm_body(x_ref, nw_ref, w_ref, o_ref, xn_ref):
    @pl.when(pl.program_id(1) == 0)
    def _():
        x = x_ref[...]
        inv = lax.rsqrt(jnp.mean(x * x, axis=-1, keepdims=True) + RMS_EPS)
        xn_ref[...] = (x * inv * nw_ref[...]).astype(bf16)

    o_ref[...] = _dot(xn_ref[...], w_ref[...])


def _norm_matmul(x, nw, w, tm, tn):
    n = x.shape[0]
    c = w.shape[1]
    return pl.pallas_call(
        _nmm_body,
        grid=(n // tm, c // tn),
        in_specs=[
            pl.BlockSpec((tm, D_MODEL), lambda i, j: (i, 0)),
            pl.BlockSpec((1, D_MODEL), lambda i, j: (0, 0)),
            pl.BlockSpec((D_MODEL, tn), lambda i, j: (0, j)),
        ],
        out_specs=pl.BlockSpec((tm, tn), lambda i, j: (i, j)),
        out_shape=jax.ShapeDtypeStruct((n, c), f32),
        scratch_shapes=[pltpu.VMEM((tm, D_MODEL), bf16)],
        compiler_params=_cparams(("parallel", "arbitrary")),
        name="norm_matmul",
    )(x, nw, w)


def _qkv_body(q_ref, kv_ref, qw_ref, kw_ref, bf_ref, seg_ref, segt_ref, u_ref,
              qo_ref, ko_ref, vo_ref, lf_ref, dtp_ref, ct_ref, carry_ref):
    @pl.when(pl.program_id(1) == 0)
    def _():
        carry_ref[...] = jnp.zeros_like(carry_ref)

    seg = seg_ref[...]
    segt = segt_ref[...]

    q = q_ref[...]
    msq = _dot2_lhs(q * q, seg)
    fac = _dot2_lhs(lax.rsqrt(msq + RMS_EPS), segt)
    qo_ref[...] = (q * fac * qw_ref[...]).astype(bf16)

    kv = kv_ref[...]
    k = kv[:, 0:KV_DIM]
    msk = _dot2_lhs(k * k, seg[0:KV_DIM, :])
    fack = _dot2_lhs(lax.rsqrt(msk + RMS_EPS), segt[:, 0:KV_DIM])
    ko_ref[...] = k * fack * kw_ref[...]
    vo_ref[...] = kv[:, KV_DIM:2 * KV_DIM]
    dtp_ref[...] = kv[:, 2 * KV_DIM:2 * KV_DIM + 128]
    f_raw = kv[:, 2 * KV_DIM + SSD_HEADS:2 * KV_DIM + SSD_HEADS + ATT_HEADS]
    lf = jax.nn.log_sigmoid(f_raw + bf_ref[...])
    lf_ref[...] = lf

    u = u_ref[...]
    p0, p1, p2 = _split3(lf)
    ct = _dot_tn(p0, u) + _dot_tn(p1, u) + _dot_tn(p2, u) + carry_ref[...]
    ct_ref[...] = ct
    carry_ref[...] = ct[:, ct.shape[1] - 1:ct.shape[1]]


def _qkv_post(main3, qw, kw, b_f, tl):
    bsz, seqlen, _ = main3.shape
    seg = np.zeros((ATT_DIM, 128), np.float32)
    for h in range(ATT_HEADS):
        seg[h * ATT_HEAD_DIM:(h + 1) * ATT_HEAD_DIM, h] = 1.0 / ATT_HEAD_DIM
    segt = (seg.T > 0).astype(np.float32)
    u = np.triu(np.ones((tl, tl), np.float32))
    outs = pl.pallas_call(
        _qkv_body,
        grid=(bsz, seqlen // tl),
        in_specs=[
            pl.BlockSpec((None, tl, 1024), lambda b, i: (b, i, COL_Q // 1024)),
            pl.BlockSpec((None, tl, 1024), lambda b, i: (b, i, COL_KV // 1024)),
            pl.BlockSpec((1, ATT_DIM), lambda b, i: (0, 0)),
            pl.BlockSpec((1, KV_DIM), lambda b, i: (0, 0)),
            pl.BlockSpec((1, ATT_HEADS), lambda b, i: (0, 0)),
            pl.BlockSpec((ATT_DIM, 128), lambda b, i: (0, 0)),
            pl.BlockSpec((128, ATT_DIM), lambda b, i: (0, 0)),
            pl.BlockSpec((tl, tl), lambda b, i: (0, 0)),
        ],
        out_specs=[
            pl.BlockSpec((None, tl, ATT_DIM), lambda b, i: (b, i, 0)),
            pl.BlockSpec((None, tl, KV_DIM), lambda b, i: (b, i, 0)),
            pl.BlockSpec((None, tl, KV_DIM), lambda b, i: (b, i, 0)),
            pl.BlockSpec((None, tl, ATT_HEADS), lambda b, i: (b, i, 0)),
            pl.BlockSpec((None, tl, 128), lambda b, i: (b, i, 0)),
            pl.BlockSpec((None, ATT_HEADS, tl), lambda b, i: (b, 0, i)),
        ],
        out_shape=[
            jax.ShapeDtypeStruct((bsz, seqlen, ATT_DIM), bf16),
            jax.ShapeDtypeStruct((bsz, seqlen, KV_DIM), f32),
            jax.ShapeDtypeStruct((bsz, seqlen, KV_DIM), f32),
            jax.ShapeDtypeStruct((bsz, seqlen, ATT_HEADS), f32),
            jax.ShapeDtypeStruct((bsz, seqlen, 128), f32),
            jax.ShapeDtypeStruct((bsz, ATT_HEADS, seqlen), f32),
        ],
        scratch_shapes=[pltpu.VMEM((ATT_HEADS, 1), f32)],
        compiler_params=_cparams(("parallel", "arbitrary")),
        name="qkv_post",
    )(main3, main3, qw, kw, b_f, jnp.asarray(seg, bf16), jnp.asarray(segt, bf16), jnp.asarray(u, bf16))
    return outs


def _ssd_body(xbc_ref, z_ref, dtp_ref, prev_ref, h0_ref, cw_ref, cb_ref, dtb_ref, alog_ref, dskip_ref,
              nrm_ref, e_ref, et_ref, tril_ref, triu_ref, y_ref, hnew_ref, cnew_ref, xpad_ref, hs_ref, *, qr, q):
    c = pl.program_id(1)
    last = pl.num_programs(1) - 1

    @pl.when(c == 0)
    def _():
        xpad_ref[0:8, :] = jnp.zeros((8, CONV_DIM), f32)
        xpad_ref[8 - (D_CONV - 1):8, :] = prev_ref[...]
        hs_ref[...] = h0_ref[...]

    x = xbc_ref[...]
    xpad_ref[8:8 + qr, :] = x
    conv = cb_ref[...] + cw_ref[D_CONV - 1:D_CONV, :] * x
    for j in range(D_CONV - 1):
        conv = conv + cw_ref[j:j + 1, :] * xpad_ref[5 + j:5 + j + qr, :]
    xc = conv * jax.nn.sigmoid(conv)

    @pl.when(c == last)
    def _():
        cnew_ref[...] = xpad_ref[qr + 5:qr + 8, :]

    xpad_ref[0:8, :] = xpad_ref[qr:qr + 8, :]

    dt = jax.nn.softplus(dtp_ref[...][:, 0:SSD_HEADS] + dtb_ref[...])
    zz = z_ref[...]
    if q != qr:
        xc = jnp.concatenate([xc, jnp.zeros((q - qr, CONV_DIM), f32)], axis=0)
        dt = jnp.concatenate([dt, jnp.zeros((q - qr, SSD_HEADS), f32)], axis=0)
        zz = jnp.concatenate([zz, jnp.zeros((q - qr, D_SSM), f32)], axis=0)
    xs = xc[:, 0:D_SSM]
    bm = xc[:, D_SSM:D_SSM + SSD_GROUPS * D_STATE].astype(bf16)
    cm = xc[:, D_SSM + SSD_GROUPS * D_STATE:CONV_DIM].astype(bf16)

    a = dt * (-jnp.exp(alog_ref[...]))
    tril = tril_ref[...]
    a0, a1, a2 = _split3(a)
    acs = _dot(tril, a0) + _dot(tril, a1) + _dot(tril, a2)
    triu = triu_ref[...]
    acst = _dot_tn(a0, triu) + _dot_tn(a1, triu) + _dot_tn(a2, triu)
    acs_last = acs[q - 1:q, :]
    e = e_ref[...]
    xdt = xs * _dot2_lhs(dt, e)
    xdtw = (xdt * _dot2_lhs(jnp.exp(acs_last - acs), e)).astype(bf16)
    eacs_e = _dot2_lhs(jnp.exp(acs), e)
    dcol = jnp.exp(acst[:, q - 1:q])
    dec_rows = _dot2_lhs_rhs(et_ref[...], jnp.broadcast_to(dcol, (SSD_HEADS, D_STATE)))

    li = lax.broadcasted_iota(jnp.int32, (q, q), 0)
    si = lax.broadcasted_iota(jnp.int32, (q, q), 1)
    causal = li >= si
    lane = lax.broadcasted_iota(jnp.int32, (q, 2 * SSD_HEAD_DIM), 1)
    left = lane < SSD_HEAD_DIM

    y_groups = []
    st_groups = []
    for g in range(SSD_GROUPS):
        bg = bm[:, g * D_STATE:(g + 1) * D_STATE]
        cg = cm[:, g * D_STATE:(g + 1) * D_STATE]
        cbm = _dot_nt(cg, bg)
        r0 = g * HEADS_PER_GROUP * SSD_HEAD_DIM
        hg = hs_ref[r0:r0 + HEADS_PER_GROUP * SSD_HEAD_DIM, :].astype(bf16)
        y_off = _dot_nt(cg, hg)
        st_groups.append(_dot_tn(xdtw[:, r0:r0 + HEADS_PER_GROUP * SSD_HEAD_DIM], bg))
        pairs = []
        for pr in range(HEADS_PER_GROUP // 2):
            h0i = g * HEADS_PER_GROUP + 2 * pr
            xpair = xdt[:, h0i * SSD_HEAD_DIM:(h0i + 2) * SSD_HEAD_DIM]
            acc = None
            for t in range(2):
                h = h0i + t
                sg = acs[:, h:h + 1] - acst[h:h + 1, :]
                m = (cbm * jnp.exp(jnp.where(causal, sg, NEG_BIG))).astype(bf16)
                xm = jnp.where(left if t == 0 else jnp.logical_not(left), xpair, 0.0).astype(bf16)
                d = _dot(m, xm)
                acc = d if acc is None else acc + d
            pairs.append(acc)
        y_diag = jnp.concatenate(pairs, axis=1)
        y_groups.append(y_diag + y_off * eacs_e[:, r0:r0 + HEADS_PER_GROUP * SSD_HEAD_DIM])
    y = jnp.concatenate(y_groups, axis=1) + dskip_ref[...] * xs
    hs_new = hs_ref[...] * dec_rows + jnp.concatenate(st_groups, axis=0)
    hs_ref[...] = hs_new

    @pl.when(c == last)
    def _():
        hnew_ref[...] = hs_new

    yg = y * (zz * jax.nn.sigmoid(zz))
    gw = D_SSM // SSD_GROUPS
    outs = []
    for g in range(SSD_GROUPS):
        t = yg[:, g * gw:(g + 1) * gw]
        outs.append(t * lax.rsqrt(jnp.mean(t * t, axis=-1, keepdims=True) + RMS_EPS))
    yn = jnp.concatenate(outs, axis=1) * nrm_ref[...]
    y_ref[...] = yn[0:qr, :].astype(bf16)


def _dot2_lhs_rhs(m, x):
    hi, lo = _split2(x)
    return _dot(m, hi) + _dot(m, lo)


def _ssd(main3, dtp, conv_prev, h0, conv_w, conv_b, dt_bias, a_log, d_skip_e, ssd_norm, qr, q):
    bsz, seqlen, _ = main3.shape
    nc = seqlen // qr
    e = np.zeros((SSD_HEADS, D_SSM), np.float32)
    for h in range(SSD_HEADS):
        e[h, h * SSD_HEAD_DIM:(h + 1) * SSD_HEAD_DIM] = 1.0
    tril = np.tril(np.ones((q, q), np.float32))
    nst = SSD_HEADS * SSD_HEAD_DIM
    full = lambda shp: pl.BlockSpec(shp, lambda b, c: tuple(0 for _ in shp))
    y, hnew, cnew = pl.pallas_call(
        functools.partial(_ssd_body, qr=qr, q=q),
        grid=(bsz, nc),
        in_specs=[
            pl.BlockSpec((None, qr, CONV_DIM), lambda b, c: (b, c, COL_XBC // CONV_DIM)),
            pl.BlockSpec((None, qr, D_SSM), lambda b, c: (b, c, COL_Z // D_SSM)),
            pl.BlockSpec((None, qr, 128), lambda b, c: (b, c, 0)),
            pl.BlockSpec((None, D_CONV - 1, CONV_DIM), lambda b, c: (b, 0, 0)),
            pl.BlockSpec((None, nst, D_STATE), lambda b, c: (b, 0, 0)),
            full((D_CONV, CONV_DIM)), full((1, CONV_DIM)), full((1, SSD_HEADS)), full((1, SSD_HEADS)),
            full((1, D_SSM)), full((1, D_SSM)), full((SSD_HEADS, D_SSM)), full((D_SSM, SSD_HEADS)),
            full((q, q)), full((q, q)),
        ],
        out_specs=[
            pl.BlockSpec((None, qr, D_SSM), lambda b, c: (b, c, 0)),
            pl.BlockSpec((None, nst, D_STATE), lambda b, c: (b, 0, 0)),
            pl.BlockSpec((None, D_CONV - 1, CONV_DIM), lambda b, c: (b, 0, 0)),
        ],
        out_shape=[
            jax.ShapeDtypeStruct((bsz, seqlen, D_SSM), bf16),
            jax.ShapeDtypeStruct((bsz, nst, D_STATE), f32),
            jax.ShapeDtypeStruct((bsz, D_CONV - 1, CONV_DIM), f32),
        ],
        scratch_shapes=[pltpu.VMEM((qr + 8, CONV_DIM), f32), pltpu.VMEM((nst, D_STATE), f32)],
        compiler_params=_cparams(("parallel", "arbitrary")),
        name="ssd",
    )(main3, main3, dtp, conv_prev, h0, conv_w, conv_b, dt_bias, a_log, d_skip_e, ssd_norm,
      jnp.asarray(e, bf16), jnp.asarray(e.T, bf16), jnp.asarray(tril, bf16), jnp.asarray(tril.T, bf16))
    return y, hnew, cnew


def _flash_body(q_ref, k_ref, v_ref, c_ref, o_ref, m_ref, l_ref, acc_ref, *, tq, tk):
    qi = pl.program_id(2)
    ki = pl.program_id(3)

    @pl.when(ki == 0)
    def _():
        m_ref[...] = jnp.full_like(m_ref, NEG_BIG)
        l_ref[...] = jnp.zeros_like(l_ref)
        acc_ref[...] = jnp.zeros_like(acc_ref)

    @pl.when(ki * tk <= qi * tq + tq - 1)
    def _():
        q = q_ref[...].reshape(Q_PER_KV * tq, ATT_HEAD_DIM)
        s = _dot_nt(q, k_ref[...])
        s3 = s.reshape(Q_PER_KV, tq, tk) - c_ref[...][:, None, :]
        row = qi * tq + lax.broadcasted_iota(jnp.int32, (tq, tk), 0)
        col = ki * tk + lax.broadcasted_iota(jnp.int32, (tq, tk), 1)
        s3 = jnp.where((col <= row)[None], s3, NEG_BIG)
        s = s3.reshape(Q_PER_KV * tq, tk)
        m_prev = m_ref[...]
        m_new = jnp.maximum(m_prev, jnp.max(s, axis=-1, keepdims=True))
        alpha = jnp.exp(m_prev - m_new)
        p = jnp.exp(s - m_new)
        l_ref[...] = alpha * l_ref[...] + jnp.sum(p, axis=-1, keepdims=True)
        acc_ref[...] = alpha * acc_ref[...] + _dot(p.astype(bf16), v_ref[...])
        m_ref[...] = m_new

    @pl.when(ki == pl.num_programs(3) - 1)
    def _():
        o = (acc_ref[...] / l_ref[...]).reshape(Q_PER_KV, tq, ATT_HEAD_DIM)
        o_ref[...] = jnp.concatenate([o[j] for j in range(Q_PER_KV)], axis=1).astype(bf16)


def _flash_fox(q5, k4, v4, c4, tq, tk):
    bsz, _, _, seqlen, _ = q5.shape
    nq, nk = seqlen // tq, seqlen // tk
    kmap = lambda b, g, i, j: (b, g, jnp.minimum(j, (i * tq + tq - 1) // tk), 0)
    return pl.pallas_call(
        functools.partial(_flash_body, tq=tq, tk=tk),
        grid=(bsz, KV_HEADS, nq, nk),
        in_specs=[
            pl.BlockSpec((None, None, Q_PER_KV, tq, ATT_HEAD_DIM), lambda b, g, i, j: (b, g, 0, i, 0)),
            pl.BlockSpec((None, None, tk, ATT_HEAD_DIM), kmap),
            pl.BlockSpec((None, None, tk, ATT_HEAD_DIM), kmap),
            pl.BlockSpec((None, None, Q_PER_KV, tk),
                         lambda b, g, i, j: (b, g, 0, jnp.minimum(j, (i * tq + tq - 1) // tk))),
        ],
        out_specs=pl.BlockSpec((None, tq, Q_PER_KV * ATT_HEAD_DIM), lambda b, g, i, j: (b, i, g)),
        out_shape=jax.ShapeDtypeStruct((bsz, seqlen, ATT_DIM), bf16),
        scratch_shapes=[
            pltpu.VMEM((Q_PER_KV * tq, 1), f32),
            pltpu.VMEM((Q_PER_KV * tq, 1), f32),
            pltpu.VMEM((Q_PER_KV * tq, ATT_HEAD_DIM), f32),
        ],
        compiler_params=_cparams(("parallel", "parallel", "parallel", "arbitrary")),
        name="flash_fox",
    )(q5, k4, v4, c4)


PAGES_PER_STEP = 8


def _lane_cumsum(x):
    n = x.shape[1]
    lane = lax.broadcasted_iota(jnp.int32, x.shape, 1)
    sh = 1
    while sh < n:
        x = x + jnp.where(lane >= sh, pltpu.roll(x, sh, axis=1), 0.0)
        sh *= 2
    return x


def _paged_body(pt_ref, qbd_ref, kn_ref, vn_ref, lfn_ref, eye_ref, *rest, dec_seq):
    npg = PAGES_PER_STEP
    k_refs = rest[0:npg]
    v_refs = rest[npg:2 * npg]
    lf_refs = rest[2 * npg:3 * npg]
    o_ref, m_ref, l_ref, acc_ref, carry_ref = rest[3 * npg:]
    j = pl.program_id(1)
    rows = qbd_ref.shape[0]

    @pl.when(j == 0)
    def _():
        m_ref[...] = jnp.full_like(m_ref, NEG_BIG)
        l_ref[...] = jnp.zeros_like(l_ref)
        acc_ref[...] = jnp.zeros_like(acc_ref)
        carry_ref[...] = jnp.zeros_like(carry_ref)

    qbd = qbd_ref[...]
    eye = eye_ref[...]

    def attend(k, v, lf, mask):
        nkeys = k.shape[0]
        p0, p1, p2 = _split3(lf)
        lft = _dot_nt(eye, p0) + _dot_nt(eye, p1) + _dot_nt(eye, p2)
        ct = _lane_cumsum(lft) + carry_ref[...]
        carry_ref[...] = ct[:, nkeys - 1:nkeys]
        s = _dot_nt(qbd, k.astype(bf16))
        s3 = s.reshape(ATT_HEADS, rows // ATT_HEADS, nkeys) - ct[:, None, :]
        s = s3.reshape(rows, nkeys)
        if mask is not None:
            s = jnp.where(mask, s, NEG_BIG)
        m_prev = m_ref[...]
        m_new = jnp.maximum(m_prev, jnp.max(s, axis=-1, keepdims=True))
        alpha = jnp.exp(m_prev - m_new)
        p = jnp.exp(s - m_new)
        l_ref[...] = alpha * l_ref[...] + jnp.sum(p, axis=-1, keepdims=True)
        acc_ref[...] = alpha * acc_ref[...] + _dot(p.astype(bf16), v.astype(bf16))
        m_ref[...] = m_new

    k8 = jnp.concatenate([r[...] for r in k_refs], axis=0)
    v8 = jnp.concatenate([r[...] for r in v_refs], axis=0)
    lf8 = jnp.concatenate([r[...] for r in lf_refs], axis=0)
    attend(k8, v8, lf8, None)

    @pl.when(j == pl.num_programs(1) - 1)
    def _():
        nk = kn_ref.shape[0]
        r = lax.broadcasted_iota(jnp.int32, (rows, nk), 0) % dec_seq
        t = lax.broadcasted_iota(jnp.int32, (rows, nk), 1)
        attend(kn_ref[...], vn_ref[...], lfn_ref[...], t <= r)
        o_ref[...] = acc_ref[...] / l_ref[...]


def _paged_fox(page_table, qbd, kn, vn, lfn, ck, cv, clf, dec_seq):
    bsz, n_pages = page_table.shape
    rows = qbd.shape[1]
    page = ck.shape[1]
    npg = PAGES_PER_STEP
    pmap = lambda i: (lambda b, j, pt: (pt[b, j * npg + i], 0, 0))
    bmap = lambda b, j, pt: (b, 0, 0)
    in_specs = [
        pl.BlockSpec((None, rows, KV_DIM), bmap),
        pl.BlockSpec((None, kn.shape[1], KV_DIM), bmap),
        pl.BlockSpec((None, vn.shape[1], KV_DIM), bmap),
        pl.BlockSpec((None, lfn.shape[1], ATT_HEADS), bmap),
        pl.BlockSpec((ATT_HEADS, ATT_HEADS), lambda b, j, pt: (0, 0)),
    ]
    in_specs += [pl.BlockSpec((None, page, KV_DIM), pmap(i)) for i in range(npg)]
    in_specs += [pl.BlockSpec((None, page, KV_DIM), pmap(i)) for i in range(npg)]
    in_specs += [pl.BlockSpec((None, page, ATT_HEADS), pmap(i)) for i in range(npg)]
    grid_spec = pltpu.PrefetchScalarGridSpec(
        num_scalar_prefetch=1,
        grid=(bsz, n_pages // npg),
        in_specs=in_specs,
        out_specs=pl.BlockSpec((None, rows, KV_DIM), bmap),
        scratch_shapes=[
            pltpu.VMEM((rows, 1), f32), pltpu.VMEM((rows, 1), f32), pltpu.VMEM((rows, KV_DIM), f32),
            pltpu.VMEM((ATT_HEADS, 1), f32),
        ],
    )
    eye = jnp.asarray(np.eye(ATT_HEADS, dtype=np.float32), bf16)
    return pl.pallas_call(
        functools.partial(_paged_body, dec_seq=dec_seq),
        grid_spec=grid_spec,
        out_shape=jax.ShapeDtypeStruct((bsz, rows, KV_DIM), f32),
        compiler_params=_cparams(("parallel", "arbitrary")),
        name="paged_fox",
    )(page_table, qbd, kn, vn, lfn, eye, *([ck] * npg), *([cv] * npg), *([clf] * npg))


def _merge_body(ys_ref, ya_ref, gs_ref, ga_ref, h_ref, ws_ref, wa_ref, wo_ref, o_ref):
    a = jax.nn.sigmoid(gs_ref[...]) * _dot(ys_ref[...], ws_ref[...])
    b = jax.nn.sigmoid(ga_ref[...]) * _dot(ya_ref[...], wa_ref[...])
    o_ref[...] = h_ref[...] + _dot((a + b).astype(bf16), wo_ref[...])


def _merge(y_ssd, y_att, main, h, ws, wa, wo, tm):
    n = h.shape[0]
    full = lambda shp: pl.BlockSpec(shp, lambda i: (0, 0))
    return pl.pallas_call(
        _merge_body,
        grid=(n // tm,),
        in_specs=[
            pl.BlockSpec((tm, D_SSM), lambda i: (i, 0)),
            pl.BlockSpec((tm, ATT_DIM), lambda i: (i, 0)),
            pl.BlockSpec((tm, D_MODEL), lambda i: (i, COL_GS // D_MODEL)),
            pl.BlockSpec((tm, D_MODEL), lambda i: (i, COL_GA // D_MODEL)),
            pl.BlockSpec((tm, D_MODEL), lambda i: (i, 0)),
            full((D_SSM, D_MODEL)), full((ATT_DIM, D_MODEL)), full((D_MODEL, D_MODEL)),
        ],
        out_specs=pl.BlockSpec((tm, D_MODEL), lambda i: (i, 0)),
        out_shape=jax.ShapeDtypeStruct((n, D_MODEL), f32),
        compiler_params=_cparams(("parallel",)),
        name="merge",
    )(y_ssd, y_att, main, main, h, ws, wa, wo)


def _pack_w_in(w_in):
    o = np.cumsum([0, D_SSM, CONV_DIM, SSD_HEADS, ATT_DIM, KV_DIM, KV_DIM, ATT_HEADS, D_MODEL, D_MODEL])
    z, xbc, dt, q, k, v, f, gs, ga = (w_in[:, int(o[i]):int(o[i + 1])] for i in range(9))
    pad = jnp.zeros((D_MODEL, COL_XBC - COL_KV - 2 * KV_DIM - SSD_HEADS - ATT_HEADS), w_in.dtype)
    w = jnp.concatenate([z, gs, ga, q, k, v, dt, f, pad, xbc], axis=1)
    return w.astype(bf16)


def _dense_pre(x2, p, tm):
    h = _ffn(x2, p["ffn1_norm"], p["ffn1_wg"], p["ffn1_wu"], p["ffn1_wd"], tm)
    main = _norm_matmul(h, p["mix_norm"], p["w_main"], tm, W_MAIN_COLS // 4)
    return h, main


def _dense_post(y_ssd, y_att, main, h, p, tm):
    h2 = _merge(y_ssd, y_att, main, h, p["w_ssd_proj"], p["w_attn_proj"], p["w_out"], tm)
    return _ffn(h2, p["ffn2_norm"], p["ffn2_wg"], p["ffn2_wu"], p["ffn2_wd"], tm)


def _ssd_args(p):
    return (p["conv_w"], p["conv_b"], p["dt_bias"], p["a_log"], p["d_skip_e"], p["ssd_norm"])


def _prompt_layer(x, p):
    bsz, seqlen, _ = x.shape
    n = bsz * seqlen
    tm = 512
    h, main = _dense_pre(x.reshape(n, D_MODEL), p, tm)
    main3 = main.reshape(bsz, seqlen, W_MAIN_COLS)
    q_bf, k, v, logf, dtp, ct = _qkv_post(main3, p["q_norm_e"], p["k_norm_e"], p["b_f"], 512)
    conv0 = jnp.zeros((bsz, D_CONV - 1, CONV_DIM), f32)
    h0 = jnp.zeros((bsz, SSD_HEADS * SSD_HEAD_DIM, D_STATE), f32)
    y_ssd, h_new, conv_new = _ssd(main3, dtp, conv0, h0, *_ssd_args(p), SSD_CHUNK, SSD_CHUNK)
    q5 = q_bf.reshape(bsz, seqlen, KV_HEADS, Q_PER_KV, ATT_HEAD_DIM).transpose(0, 2, 3, 1, 4)
    k4 = k.astype(bf16).reshape(bsz, seqlen, KV_HEADS, ATT_HEAD_DIM).transpose(0, 2, 1, 3)
    v4 = v.astype(bf16).reshape(bsz, seqlen, KV_HEADS, ATT_HEAD_DIM).transpose(0, 2, 1, 3)
    c4 = ct.reshape(bsz, KV_HEADS, Q_PER_KV, seqlen)
    y_att = _flash_fox(q5, k4, v4, c4, 256, 512)
    y = _dense_post(y_ssd.reshape(n, D_SSM), y_att.reshape(n, ATT_DIM), main, h, p, tm)
    return (y.reshape(bsz, seqlen, D_MODEL),
            k.reshape(bsz, seqlen, KV_HEADS, ATT_HEAD_DIM), v.reshape(bsz, seqlen, KV_HEADS, ATT_HEAD_DIM),
            logf, h_new.reshape(bsz, SSD_HEADS, SSD_HEAD_DIM, D_STATE), conv_new)


def _sample_layer(x, conv_prev, h0, ck, cv, clf, page_table, p):
    bsz, t, _ = x.shape
    n = bsz * t
    tm = n
    h, main = _dense_pre(x.reshape(n, D_MODEL), p, tm)
    q_bf, k, v, logf, dtp, _ = _qkv_post(main.reshape(1, n, W_MAIN_COLS), p["q_norm_e"], p["k_norm_e"], p["b_f"], n)
    main3 = main.reshape(bsz, t, W_MAIN_COLS)
    y_ssd, h_new, conv_new = _ssd(main3, dtp.reshape(bsz, t, 128), conv_prev,
                                  h0.reshape(bsz, SSD_HEADS * SSD_HEAD_DIM, D_STATE), *_ssd_args(p), t, 16)
    qg = q_bf.reshape(bsz, t, KV_HEADS, Q_PER_KV, ATT_HEAD_DIM).transpose(0, 2, 3, 1, 4)
    eye = jnp.eye(KV_HEADS, dtype=bf16)
    qbd = (qg[:, :, :, :, None, :] * eye[None, :, None, None, :, None]).reshape(bsz, ATT_HEADS * t, KV_DIM)
    page = ck.shape[1]
    padk = lambda a: jnp.pad(a.reshape(bsz, t, -1), ((0, 0), (0, page - t), (0, 0)))
    o = _paged_fox(page_table, qbd, padk(k), padk(v), padk(logf),
                   ck.reshape(ck.shape[0], page, KV_DIM), cv.reshape(cv.shape[0], page, KV_DIM), clf, t)
    o6 = o.reshape(bsz, KV_HEADS, Q_PER_KV, t, KV_HEADS, ATT_HEAD_DIM)
    y_att = jnp.stack([o6[:, g, :, :, g, :] for g in range(KV_HEADS)], axis=1)
    y_att = y_att.transpose(0, 3, 1, 2, 4).reshape(n, ATT_DIM).astype(bf16)
    y = _dense_post(y_ssd.reshape(n, D_SSM), y_att, main, h, p, tm)
    return (y.reshape(bsz, t, D_MODEL),
            k.reshape(bsz, t, KV_HEADS, ATT_HEAD_DIM), v.reshape(bsz, t, KV_HEADS, ATT_HEAD_DIM),
            logf.reshape(bsz, t, ATT_HEADS), h_new.reshape(bsz, SSD_HEADS, SSD_HEAD_DIM, D_STATE), conv_new)


def kernel(x_prompt, x_sample, cache_k, cache_v, cache_logf, state_ssm, state_conv, page_table, ffn1_norm, ffn1_w_gate, ffn1_w_up, ffn1_w_down, mix_norm, w_in, conv_w, conv_b, dt_bias, a_log, d_skip, ssd_norm, q_norm, k_norm, b_f, w_ssd_proj, w_attn_proj, w_out, ffn2_norm, ffn2_w_gate, ffn2_w_up, ffn2_w_down):
    depth = w_in.shape[0]
    yp, ys = x_prompt, x_sample
    prompt_states, sample_states = [], []
    for l in range(depth):
        p = {
            "ffn1_norm": ffn1_norm[l][None], "ffn1_wg": ffn1_w_gate[l].astype(bf16),
            "ffn1_wu": ffn1_w_up[l].astype(bf16), "ffn1_wd": ffn1_w_down[l].astype(bf16),
            "mix_norm": mix_norm[l][None], "w_main": _pack_w_in(w_in[l]),
            "conv_w": conv_w[l], "conv_b": conv_b[l][None], "dt_bias": dt_bias[l][None], "a_log": a_log[l][None],
            "d_skip_e": jnp.repeat(d_skip[l], SSD_HEAD_DIM)[None], "ssd_norm": ssd_norm[l][None],
            "q_norm_e": (jnp.tile(q_norm[l], ATT_HEADS) * (ATT_HEAD_DIM ** -0.5))[None], "k_norm_e": jnp.tile(k_norm[l], KV_HEADS)[None],
            "b_f": b_f[l][None],
            "w_ssd_proj": w_ssd_proj[l].astype(bf16), "w_attn_proj": w_attn_proj[l].astype(bf16),
            "w_out": w_out[l].astype(bf16),
            "ffn2_norm": ffn2_norm[l][None], "ffn2_wg": ffn2_w_gate[l].astype(bf16),
            "ffn2_wu": ffn2_w_up[l].astype(bf16), "ffn2_wd": ffn2_w_down[l].astype(bf16),
        }
        yp, *sp = _prompt_layer(yp, p)
        ys, *ss = _sample_layer(ys, state_conv[l], state_ssm[l], cache_k[l], cache_v[l], cache_logf[l],
                                page_table, p)
        prompt_states.append(sp)
        sample_states.append(ss)
    k_p, v_p, logf_p, ssm_p, conv_p = (jnp.stack(t) for t in zip(*prompt_states))
    k_s, v_s, logf_s, ssm_s, conv_s = (jnp.stack(t) for t in zip(*sample_states))
    return (yp, ys, k_p, v_p, logf_p, ssm_p, conv_p, k_s, v_s, logf_s, ssm_s, conv_s)
```

```python
import functools
import numpy as np
import jax
import jax.numpy as jnp
from jax import lax
from jax.experimental import pallas as pl
from jax.experimental.pallas import tpu as pltpu

f32 = jnp.float32
bf16 = jnp.bfloat16

RMS_EPS = 1e-6
D_MODEL = 1024
D_FF = 2816
D_SSM = 2048
SSD_HEAD_DIM = 64
SSD_HEADS = 32
SSD_GROUPS = 4
HEADS_PER_GROUP = SSD_HEADS // SSD_GROUPS
D_STATE = 128
D_CONV = 4
CONV_DIM = D_SSM + 2 * SSD_GROUPS * D_STATE
SSD_CHUNK = 128
ATT_HEAD_DIM = 64
ATT_HEADS = 16
KV_HEADS = 4
Q_PER_KV = ATT_HEADS // KV_HEADS
ATT_DIM = ATT_HEADS * ATT_HEAD_DIM
KV_DIM = KV_HEADS * ATT_HEAD_DIM
NEG_BIG = -1e30

COL_Z = 0
COL_GS = 2048
COL_GA = 3072
COL_Q = 4096
COL_KV = 5120
COL_XBC = 6144
W_MAIN_COLS = 9216

VMEM_LIMIT = 56 * 1024 * 1024


def _cparams(sem):
    return pltpu.CompilerParams(dimension_semantics=sem, vmem_limit_bytes=VMEM_LIMIT)


def _split2(x):
    hi = x.astype(bf16)
    lo = (x - hi.astype(f32)).astype(bf16)
    return hi, lo


def _split3(x):
    hi = x.astype(bf16)
    r = x - hi.astype(f32)
    mid = r.astype(bf16)
    lo = (r - mid.astype(f32)).astype(bf16)
    return hi, mid, lo


def _dot(a, b):
    return jnp.dot(a, b, preferred_element_type=f32)


def _dot_nt(a, b):
    return lax.dot_general(a, b, (((1,), (1,)), ((), ())), preferred_element_type=f32)


def _dot_tn(a, b):
    return lax.dot_general(a, b, (((0,), (0,)), ((), ())), preferred_element_type=f32)


def _dot2_lhs(x, m):
    hi, lo = _split2(x)
    return _dot(hi, m) + _dot(lo, m)


def _ffn_body(x_ref, nw_ref, wg_ref, wu_ref, wd_ref, o_ref, xn_ref, acc_ref):
    j = pl.program_id(1)

    @pl.when(j == 0)
    def _():
        x = x_ref[...]
        inv = lax.rsqrt(jnp.mean(x * x, axis=-1, keepdims=True) + RMS_EPS)
        xn_ref[...] = (x * inv * nw_ref[...]).astype(bf16)
        acc_ref[...] = jnp.zeros_like(acc_ref)

    xn = xn_ref[...]
    g = _dot(xn, wg_ref[...])
    u = _dot(xn, wu_ref[...])
    a = (g * jax.nn.sigmoid(g)) * u
    acc_ref[...] += _dot(a.astype(bf16), wd_ref[...])

    @pl.when(j == pl.num_programs(1) - 1)
    def _():
        o_ref[...] = x_ref[...] + 0.5 * acc_ref[...]


def _ffn(x, nw, wg, wu, wd, tm):
    n = x.shape[0]
    tf = D_FF // 2
    return pl.pallas_call(
        _ffn_body,
        grid=(n // tm, D_FF // tf),
        in_specs=[
            pl.BlockSpec((tm, D_MODEL), lambda i, j: (i, 0)),
            pl.BlockSpec((1, D_MODEL), lambda i, j: (0, 0)),
            pl.BlockSpec((D_MODEL, tf), lambda i, j: (0, j)),
            pl.BlockSpec((D_MODEL, tf), lambda i, j: (0, j)),
            pl.BlockSpec((tf, D_MODEL), lambda i, j: (j, 0)),
        ],
        out_specs=pl.BlockSpec((tm, D_MODEL), lambda i, j: (i, 0)),
        out_shape=jax.ShapeDtypeStruct((n, D_MODEL), f32),
        scratch_shapes=[pltpu.VMEM((tm, D_MODEL), bf16), pltpu.VMEM((tm, D_MODEL), f32)],
        compiler_params=_cparams(("parallel", "arbitrary")),
        name="ffn",
    )(x, nw, wg, wu, wd)


def _nmm_body(x_ref, nw_ref, w_ref, o_ref, xn_ref):
    @pl.when(pl.program_id(1) == 0)
    def _():
        x = x_ref[...]
        inv = lax.rsqrt(jnp.mean(x * x, axis=-1, keepdims=True) + RMS_EPS)
        xn_ref[...] = (x * inv * nw_ref[...]).astype(bf16)

    o_ref[...] = _dot(xn_ref[...], w_ref[...])


def _norm_matmul(x, nw, w, tm, tn):
    n = x.shape[0]
    c = w.shape[1]
    return pl.pallas_call(
        _nmm_body,
        grid=(n // tm, c // tn),
        in_specs=[
            pl.BlockSpec((tm, D_MODEL), lambda i, j: (i, 0)),
            pl.BlockSpec((1, D_MODEL), lambda i, j: (0, 0)),
            pl.BlockSpec((D_MODEL, tn), lambda i, j: (0, j)),
        ],
        out_specs=pl.BlockSpec((tm, tn), lambda i, j: (i, j)),
        out_shape=jax.ShapeDtypeStruct((n, c), f32),
        scratch_shapes=[pltpu.VMEM((tm, D_MODEL), bf16)],
        compiler_params=_cparams(("parallel", "arbitrary")),
        name="norm_matmul",
    )(x, nw, w)


AUG_DIM = 128
BIAS_PIECES = 3


def _qkv_body(q_ref, kv_ref, qw_ref, kw_ref, bf_ref, seg_ref, segt_ref, tril_ref, pk_ref, pc_ref,
              qt_ref, ka_ref, vt_ref, ko_ref, vo_ref, lf_ref, dtp_ref, carry_ref):
    @pl.when(pl.program_id(1) == 0)
    def _():
        carry_ref[...] = jnp.zeros_like(carry_ref)

    seg = seg_ref[...]
    segt = segt_ref[...]
    tl = q_ref.shape[0]

    q = q_ref[...]
    msq = _dot2_lhs(q * q, seg)
    fac = _dot2_lhs(lax.rsqrt(msq + RMS_EPS), segt)
    qnt = (q * fac * qw_ref[...]).T
    row = lax.broadcasted_iota(jnp.int32, (AUG_DIM - ATT_HEAD_DIM, tl), 0)
    for h in range(ATT_HEADS):
        qt_ref[h, 0:ATT_HEAD_DIM, :] = qnt[h * ATT_HEAD_DIM:(h + 1) * ATT_HEAD_DIM, :].astype(bf16)
        lo = BIAS_PIECES * (h % Q_PER_KV)
        qt_ref[h, ATT_HEAD_DIM:AUG_DIM, :] = jnp.where((row >= lo) & (row < lo + BIAS_PIECES), 1.0, 0.0).astype(bf16)

    kv = kv_ref[...]
    k = kv[:, 0:KV_DIM]
    msk = _dot2_lhs(k * k, seg[0:KV_DIM, :])
    fack = _dot2_lhs(lax.rsqrt(msk + RMS_EPS), segt[:, 0:KV_DIM])
    kn = k * fack * kw_ref[...]
    ko_ref[...] = kn
    v = kv[:, KV_DIM:2 * KV_DIM]
    vo_ref[...] = v
    vt_ref[...] = v.T.astype(bf16)
    dtp_ref[...] = kv[:, 2 * KV_DIM:2 * KV_DIM + 128]
    f_raw = kv[:, 2 * KV_DIM + SSD_HEADS:2 * KV_DIM + SSD_HEADS + ATT_HEADS]
    lf = jax.nn.log_sigmoid(f_raw + bf_ref[...])
    lf_ref[...] = lf

    tril = tril_ref[...]
    p0, p1, p2 = _split3(lf)
    c = _dot(tril, p0) + _dot(tril, p1) + _dot(tril, p2) + carry_ref[...]
    carry_ref[...] = c[tl - 1:tl, :]
    n0, n1, n2 = _split3(-c)
    ka = _dot(kn.astype(bf16), pk_ref[...]) + _dot(n0, pc_ref[0]) + _dot(n1, pc_ref[1]) + _dot(n2, pc_ref[2])
    for g in range(KV_HEADS):
        ka_ref[g] = ka[:, g * AUG_DIM:(g + 1) * AUG_DIM].astype(bf16)


def _qkv_post(main3, qw, kw, b_f, tl):
    bsz, seqlen, _ = main3.shape
    seg = np.zeros((ATT_DIM, 128), np.float32)
    for h in range(ATT_HEADS):
        seg[h * ATT_HEAD_DIM:(h + 1) * ATT_HEAD_DIM, h] = 1.0 / ATT_HEAD_DIM
    segt = (seg.T > 0).astype(np.float32)
    tril = np.tril(np.ones((tl, tl), np.float32))
    pk = np.zeros((KV_DIM, KV_HEADS * AUG_DIM), np.float32)
    pc = np.zeros((BIAS_PIECES, ATT_HEADS, KV_HEADS * AUG_DIM), np.float32)
    for g in range(KV_HEADS):
        for d in range(ATT_HEAD_DIM):
            pk[g * ATT_HEAD_DIM + d, g * AUG_DIM + d] = 1.0
        for j in range(Q_PER_KV):
            for i in range(BIAS_PIECES):
                pc[i, g * Q_PER_KV + j, g * AUG_DIM + ATT_HEAD_DIM + BIAS_PIECES * j + i] = 1.0
    full = lambda shp: pl.BlockSpec(shp, lambda b, i: tuple(0 for _ in shp))
    outs = pl.pallas_call(
        _qkv_body,
        grid=(bsz, seqlen // tl),
        in_specs=[
            pl.BlockSpec((None, tl, 1024), lambda b, i: (b, i, COL_Q // 1024)),
            pl.BlockSpec((None, tl, 1024), lambda b, i: (b, i, COL_KV // 1024)),
            full((1, ATT_DIM)), full((1, KV_DIM)), full((1, ATT_HEADS)),
            full((ATT_DIM, 128)), full((128, ATT_DIM)), full((tl, tl)),
            full((KV_DIM, KV_HEADS * AUG_DIM)), full((BIAS_PIECES, ATT_HEADS, KV_HEADS * AUG_DIM)),
        ],
        out_specs=[
            pl.BlockSpec((None, ATT_HEADS, AUG_DIM, tl), lambda b, i: (b, 0, 0, i)),
            pl.BlockSpec((None, KV_HEADS, tl, AUG_DIM), lambda b, i: (b, 0, i, 0)),
            pl.BlockSpec((None, KV_DIM, tl), lambda b, i: (b, 0, i)),
            pl.BlockSpec((None, tl, KV_DIM), lambda b, i: (b, i, 0)),
            pl.BlockSpec((None, tl, KV_DIM), lambda b, i: (b, i, 0)),
            pl.BlockSpec((None, tl, ATT_HEADS), lambda b, i: (b, i, 0)),
            pl.BlockSpec((None, tl, 128), lambda b, i: (b, i, 0)),
        ],
        out_shape=[
            jax.ShapeDtypeStruct((bsz, ATT_HEADS, AUG_DIM, seqlen), bf16),
            jax.ShapeDtypeStruct((bsz, KV_HEADS, seqlen, AUG_DIM), bf16),
            jax.ShapeDtypeStruct((bsz, KV_DIM, seqlen), bf16),
            jax.ShapeDtypeStruct((bsz, seqlen, KV_DIM), f32),
            jax.ShapeDtypeStruct((bsz, seqlen, KV_DIM), f32),
            jax.ShapeDtypeStruct((bsz, seqlen, ATT_HEADS), f32),
            jax.ShapeDtypeStruct((bsz, seqlen, 128), f32),
        ],
        scratch_shapes=[pltpu.VMEM((1, ATT_HEADS), f32)],
        compiler_params=_cparams(("parallel", "arbitrary")),
        name="qkv_post",
    )(main3, main3, qw, kw, b_f, jnp.asarray(seg, bf16), jnp.asarray(segt, bf16), jnp.asarray(tril, bf16),
      jnp.asarray(pk, bf16), jnp.asarray(pc, bf16))
    return outs


def _ssd_body(xbc_ref, z_ref, dtp_ref, prev_ref, h0_ref, cw_ref, cb_ref, dtb_ref, alog_ref, dskip_ref,
              nrm_ref, e_ref, et_ref, tril_ref, triu_ref, y_ref, hnew_ref, cnew_ref, xpad_ref, hs_ref, *, qr, q):
    c = pl.program_id(1)
    last = pl.num_programs(1) - 1

    @pl.when(c == 0)
    def _():
        xpad_ref[0:8, :] = jnp.zeros((8, CONV_DIM), f32)
        xpad_ref[8 - (D_CONV - 1):8, :] = prev_ref[...]
        hs_ref[...] = h0_ref[...]

    x = xbc_ref[...]
    xpad_ref[8:8 + qr, :] = x
    conv = cb_ref[...] + cw_ref[D_CONV - 1:D_CONV, :] * x
    for j in range(D_CONV - 1):
        conv = conv + cw_ref[j:j + 1, :] * xpad_ref[5 + j:5 + j + qr, :]
    xc = conv * jax.nn.sigmoid(conv)

    @pl.when(c == last)
    def _():
        cnew_ref[...] = xpad_ref[qr + 5:qr + 8, :]

    xpad_ref[0:8, :] = xpad_ref[qr:qr + 8, :]

    dt = jax.nn.softplus(dtp_ref[...][:, 0:SSD_HEADS] + dtb_ref[...])
    zz = z_ref[...]
    if q != qr:
        xc = jnp.concatenate([xc, jnp.zeros((q - qr, CONV_DIM), f32)], axis=0)
        dt = jnp.concatenate([dt, jnp.zeros((q - qr, SSD_HEADS), f32)], axis=0)
        zz = jnp.concatenate([zz, jnp.zeros((q - qr, D_SSM), f32)], axis=0)
    xs = xc[:, 0:D_SSM]
    bm = xc[:, D_SSM:D_SSM + SSD_GROUPS * D_STATE].astype(bf16)
    cm = xc[:, D_SSM + SSD_GROUPS * D_STATE:CONV_DIM].astype(bf16)

    a = dt * (-jnp.exp(alog_ref[...]))
    tril = tril_ref[...]
    a0, a1, a2 = _split3(a)
    acs = _dot(tril, a0) + _dot(tril, a1) + _dot(tril, a2)
    triu = triu_ref[...]
    acst = _dot_tn(a0, triu) + _dot_tn(a1, triu) + _dot_tn(a2, triu)
    acs_last = acs[q - 1:q, :]
    e = e_ref[...]
    xdt = xs * _dot2_lhs(dt, e)
    xdtw = (xdt * _dot2_lhs(jnp.exp(acs_last - acs), e)).astype(bf16)
    eacs_e = _dot2_lhs(jnp.exp(acs), e)
    dcol = jnp.exp(acst[:, q - 1:q])
    dec_rows = _dot2_lhs_rhs(et_ref[...], jnp.broadcast_to(dcol, (SSD_HEADS, D_STATE)))

    li = lax.broadcasted_iota(jnp.int32, (q, q), 0)
    si = lax.broadcasted_iota(jnp.int32, (q, q), 1)
    causal = li >= si
    lane = lax.broadcasted_iota(jnp.int32, (q, 2 * SSD_HEAD_DIM), 1)
    left = lane < SSD_HEAD_DIM

    y_groups = []
    st_groups = []
    for g in range(SSD_GROUPS):
        bg = bm[:, g * D_STATE:(g + 1) * D_STATE]
        cg = cm[:, g * D_STATE:(g + 1) * D_STATE]
        cbm = _dot_nt(cg, bg)
        r0 = g * HEADS_PER_GROUP * SSD_HEAD_DIM
        hg = hs_ref[r0:r0 + HEADS_PER_GROUP * SSD_HEAD_DIM, :].astype(bf16)
        y_off = _dot_nt(cg, hg)
        st_groups.append(_dot_tn(xdtw[:, r0:r0 + HEADS_PER_GROUP * SSD_HEAD_DIM], bg))
        pairs = []
        for pr in range(HEADS_PER_GROUP // 2):
            h0i = g * HEADS_PER_GROUP + 2 * pr
            xpair = xdt[:, h0i * SSD_HEAD_DIM:(h0i + 2) * SSD_HEAD_DIM]
            acc = None
            for t in range(2):
                h = h0i + t
                sg = acs[:, h:h + 1] - acst[h:h + 1, :]
                m = (cbm * jnp.exp(jnp.where(causal, sg, NEG_BIG))).astype(bf16)
                xm = jnp.where(left if t == 0 else jnp.logical_not(left), xpair, 0.0).astype(bf16)
                d = _dot(m, xm)
                acc = d if acc is None else acc + d
            pairs.append(acc)
        y_diag = jnp.concatenate(pairs, axis=1)
        y_groups.append(y_diag + y_off * eacs_e[:, r0:r0 + HEADS_PER_GROUP * SSD_HEAD_DIM])
    y = jnp.concatenate(y_groups, axis=1) + dskip_ref[...] * xs
    hs_new = hs_ref[...] * dec_rows + jnp.concatenate(st_groups, axis=0)
    hs_ref[...] = hs_new

    @pl.when(c == last)
    def _():
        hnew_ref[...] = hs_new

    yg = y * (zz * jax.nn.sigmoid(zz))
    gw = D_SSM // SSD_GROUPS
    outs = []
    for g in range(SSD_GROUPS):
        t = yg[:, g * gw:(g + 1) * gw]
        outs.append(t * lax.rsqrt(jnp.mean(t * t, axis=-1, keepdims=True) + RMS_EPS))
    yn = jnp.concatenate(outs, axis=1) * nrm_ref[...]
    y_ref[...] = yn[0:qr, :].astype(bf16)


def _dot2_lhs_rhs(m, x):
    hi, lo = _split2(x)
    return _dot(m, hi) + _dot(m, lo)


def _ssd(main3, dtp, conv_prev, h0, conv_w, conv_b, dt_bias, a_log, d_skip_e, ssd_norm, qr, q):
    bsz, seqlen, _ = main3.shape
    nc = seqlen // qr
    e = np.zeros((SSD_HEADS, D_SSM), np.float32)
    for h in range(SSD_HEADS):
        e[h, h * SSD_HEAD_DIM:(h + 1) * SSD_HEAD_DIM] = 1.0
    tril = np.tril(np.ones((q, q), np.float32))
    nst = SSD_HEADS * SSD_HEAD_DIM
    full = lambda shp: pl.BlockSpec(shp, lambda b, c: tuple(0 for _ in shp))
    y, hnew, cnew = pl.pallas_call(
        functools.partial(_ssd_body, qr=qr, q=q),
        grid=(bsz, nc),
        in_specs=[
            pl.BlockSpec((None, qr, CONV_DIM), lambda b, c: (b, c, COL_XBC // CONV_DIM)),
            pl.BlockSpec((None, qr, D_SSM), lambda b, c: (b, c, COL_Z // D_SSM)),
            pl.BlockSpec((None, qr, 128), lambda b, c: (b, c, 0)),
            pl.BlockSpec((None, D_CONV - 1, CONV_DIM), lambda b, c: (b, 0, 0)),
            pl.BlockSpec((None, nst, D_STATE), lambda b, c: (b, 0, 0)),
            full((D_CONV, CONV_DIM)), full((1, CONV_DIM)), full((1, SSD_HEADS)), full((1, SSD_HEADS)),
            full((1, D_SSM)), full((1, D_SSM)), full((SSD_HEADS, D_SSM)), full((D_SSM, SSD_HEADS)),
            full((q, q)), full((q, q)),
        ],
        out_specs=[
            pl.BlockSpec((None, qr, D_SSM), lambda b, c: (b, c, 0)),
            pl.BlockSpec((None, nst, D_STATE), lambda b, c: (b, 0, 0)),
            pl.BlockSpec((None, D_CONV - 1, CONV_DIM), lambda b, c: (b, 0, 0)),
        ],
        out_shape=[
            jax.ShapeDtypeStruct((bsz, seqlen, D_SSM), bf16),
            jax.ShapeDtypeStruct((bsz, nst, D_STATE), f32),
            jax.ShapeDtypeStruct((bsz, D_CONV - 1, CONV_DIM), f32),
        ],
        scratch_shapes=[pltpu.VMEM((qr + 8, CONV_DIM), f32), pltpu.VMEM((nst, D_STATE), f32)],
        compiler_params=_cparams(("parallel", "arbitrary")),
        name="ssd",
    )(main3, main3, dtp, conv_prev, h0, conv_w, conv_b, dt_bias, a_log, d_skip_e, ssd_norm,
      jnp.asarray(e, bf16), jnp.asarray(e.T, bf16), jnp.asarray(tril, bf16), jnp.asarray(tril.T, bf16))
    return y, hnew, cnew


FLASH_Q_LANES = 256
FLASH_KEY_ROWS = 128
FLASH_LOOKAHEAD = 4


def _flash_body(qi_ref, ki_ref, qt_ref, ka_ref, vt_ref, o_ref, m_ref, l_ref, acc_ref, *, tb):
    t = pl.program_id(2)
    qi = qi_ref[t]
    ki = ki_ref[t]
    ql, kr = FLASH_Q_LANES, FLASH_KEY_ROWS

    @pl.when(ki == 0)
    def _():
        m_ref[...] = jnp.full_like(m_ref, NEG_BIG)
        l_ref[...] = jnp.zeros_like(l_ref)
        acc_ref[...] = jnp.zeros_like(acc_ref)

    def process(diagonal):
        key_i = lax.broadcasted_iota(jnp.int32, (kr, ql), 0)
        qry_i = lax.broadcasted_iota(jnp.int32, (kr, ql), 1)
        tiles = [(j, r, kb) for kb in range(tb // kr) for j in range(Q_PER_KV) for r in range(tb // ql)
                 if not (diagonal and kb * kr > r * ql + ql - 1)]

        def scores(tile):
            j, r, kb = tile
            return _dot(ka_ref[kb * kr:(kb + 1) * kr, :], qt_ref[j, :, r * ql:(r + 1) * ql])

        def finish(tile, st):
            j, r, kb = tile
            cols = slice(r * ql, (r + 1) * ql)
            if diagonal and kb * kr + kr - 1 > r * ql:
                st = jnp.where(key_i + kb * kr <= qry_i + r * ql, st, NEG_BIG)
            m_prev = m_ref[j, :, cols]
            m_new = jnp.maximum(m_prev, jnp.max(st, axis=0, keepdims=True))
            alpha = jnp.exp(m_prev - m_new)
            p = jnp.exp(st - m_new)
            l_ref[j, :, cols] = alpha * l_ref[j, :, cols] + jnp.sum(p, axis=0, keepdims=True)
            acc_ref[j, :, cols] = (alpha * acc_ref[j, :, cols]
                                   + _dot(vt_ref[:, kb * kr:(kb + 1) * kr], p.astype(bf16)))
            m_ref[j, :, cols] = m_new

        pending = [scores(tl_) for tl_ in tiles[:FLASH_LOOKAHEAD]]
        for i, tile in enumerate(tiles):
            if i + FLASH_LOOKAHEAD < len(tiles):
                pending.append(scores(tiles[i + FLASH_LOOKAHEAD]))
            finish(tile, pending.pop(0))

    pl.when(ki < qi)(lambda: process(False))

    @pl.when(ki == qi)
    def _():
        process(True)
        o = acc_ref[...] / l_ref[...]
        o_ref[...] = jnp.concatenate([o[j].T for j in range(Q_PER_KV)], axis=1).astype(bf16)


def _flash_fox(qt, ka, vt, tb):
    bsz, _, _, seqlen = qt.shape
    nb = seqlen // tb
    qi_tab = np.array([i for i in range(nb) for j in range(i + 1)], np.int32)
    ki_tab = np.array([j for i in range(nb) for j in range(i + 1)], np.int32)
    grid_spec = pltpu.PrefetchScalarGridSpec(
        num_scalar_prefetch=2,
        grid=(bsz, KV_HEADS, qi_tab.shape[0]),
        in_specs=[
            pl.BlockSpec((None, Q_PER_KV, AUG_DIM, tb), lambda b, g, t, qi, ki: (b, g, 0, qi[t])),
            pl.BlockSpec((None, None, tb, AUG_DIM), lambda b, g, t, qi, ki: (b, g, ki[t], 0)),
            pl.BlockSpec((None, None, ATT_HEAD_DIM, tb), lambda b, g, t, qi, ki: (b, g, 0, ki[t])),
        ],
        out_specs=pl.BlockSpec((None, tb, Q_PER_KV * ATT_HEAD_DIM), lambda b, g, t, qi, ki: (b, qi[t], g)),
        scratch_shapes=[
            pltpu.VMEM((Q_PER_KV, 1, tb), f32),
            pltpu.VMEM((Q_PER_KV, 1, tb), f32),
            pltpu.VMEM((Q_PER_KV, ATT_HEAD_DIM, tb), f32),
        ],
    )
    return pl.pallas_call(
        functools.partial(_flash_body, tb=tb),
        grid_spec=grid_spec,
        out_shape=jax.ShapeDtypeStruct((bsz, seqlen, ATT_DIM), bf16),
        compiler_params=_cparams(("parallel", "parallel", "arbitrary")),
        name="flash_fox",
    )(jnp.asarray(qi_tab), jnp.asarray(ki_tab), qt, ka, vt)


PAGES_PER_STEP = 16
PAGES_PER_SUBBLOCK = 16


def _lane_cumsum(x):
    n = x.shape[1]
    lane = lax.broadcasted_iota(jnp.int32, x.shape, 1)
    sh = 1
    while sh < n:
        x = x + jnp.where(lane >= sh, pltpu.roll(x, sh, axis=1), 0.0)
        sh *= 2
    return x


def _paged_body(pt_ref, qbd_ref, kn_ref, vn_ref, lfn_ref, *rest, dec_seq):
    npg = PAGES_PER_STEP
    sub = PAGES_PER_SUBBLOCK
    k_refs = rest[0:npg]
    v_refs = rest[npg:2 * npg]
    lf_refs = rest[2 * npg:3 * npg]
    o_ref, m_ref, l_ref, acc_ref, carry_ref = rest[3 * npg:]
    j = pl.program_id(1)
    rows = qbd_ref.shape[0]

    @pl.when(j == 0)
    def _():
        m_ref[...] = jnp.full_like(m_ref, NEG_BIG)
        l_ref[...] = jnp.zeros_like(l_ref)
        acc_ref[...] = jnp.zeros_like(acc_ref)
        carry_ref[...] = jnp.zeros_like(carry_ref)

    qbd = qbd_ref[...]

    def attend(kt, vt, lft, mask):
        nkeys = kt.shape[1]
        ct = _lane_cumsum(lft) + carry_ref[...]
        carry_ref[...] = ct[:, nkeys - 1:nkeys]
        s = _dot(qbd, kt.astype(bf16))
        s3 = s.reshape(ATT_HEADS, rows // ATT_HEADS, nkeys) - ct[:, None, :]
        s = s3.reshape(rows, nkeys)
        if mask is not None:
            s = jnp.where(mask, s, NEG_BIG)
        m_prev = m_ref[...]
        m_new = jnp.maximum(m_prev, jnp.max(s, axis=-1, keepdims=True))
        alpha = jnp.exp(m_prev - m_new)
        p = jnp.exp(s - m_new)
        l_ref[...] = alpha * l_ref[...] + jnp.sum(p, axis=-1, keepdims=True)
        acc_ref[...] = alpha * acc_ref[...] + _dot_nt(p.astype(bf16), vt.astype(bf16))
        m_ref[...] = m_new

    for sb in range(npg // sub):
        pages = range(sb * sub, (sb + 1) * sub)
        kt = jnp.concatenate([k_refs[i][...].reshape(KV_DIM, -1) for i in pages], axis=1)
        vt = jnp.concatenate([v_refs[i][...].reshape(KV_DIM, -1) for i in pages], axis=1)
        lft = jnp.concatenate([lf_refs[i][...] for i in pages], axis=1)
        attend(kt, vt, lft, None)

    @pl.when(j == pl.num_programs(1) - 1)
    def _():
        nk = kn_ref.shape[1]
        r = lax.broadcasted_iota(jnp.int32, (rows, nk), 0) % dec_seq
        t = lax.broadcasted_iota(jnp.int32, (rows, nk), 1)
        attend(kn_ref[...], vn_ref[...], lfn_ref[...], t <= r)
        o_ref[...] = acc_ref[...] / l_ref[...]


def _paged_fox(page_table, qbd, knt, vnt, lfnt, ckt, cvt, clft, dec_seq):
    bsz, n_pages = page_table.shape
    rows = qbd.shape[1]
    page = ckt.shape[3]
    npg = PAGES_PER_STEP
    pmap4 = lambda i: (lambda b, j, pt: (pt[b, j * npg + i], 0, 0, 0))
    pmap3 = lambda i: (lambda b, j, pt: (pt[b, j * npg + i], 0, 0))
    bmap = lambda b, j, pt: (b, 0, 0)
    in_specs = [
        pl.BlockSpec((None, rows, KV_DIM), bmap),
        pl.BlockSpec((None, KV_DIM, knt.shape[2]), bmap),
        pl.BlockSpec((None, KV_DIM, vnt.shape[2]), bmap),
        pl.BlockSpec((None, ATT_HEADS, lfnt.shape[2]), bmap),
    ]
    in_specs += [pl.BlockSpec((None, KV_HEADS, ATT_HEAD_DIM, page), pmap4(i)) for i in range(npg)]
    in_specs += [pl.BlockSpec((None, KV_HEADS, ATT_HEAD_DIM, page), pmap4(i)) for i in range(npg)]
    in_specs += [pl.BlockSpec((None, ATT_HEADS, page), pmap3(i)) for i in range(npg)]
    grid_spec = pltpu.PrefetchScalarGridSpec(
        num_scalar_prefetch=1,
        grid=(bsz, n_pages // npg),
        in_specs=in_specs,
        out_specs=pl.BlockSpec((None, rows, KV_DIM), bmap),
        scratch_shapes=[
            pltpu.VMEM((rows, 1), f32), pltpu.VMEM((rows, 1), f32), pltpu.VMEM((rows, KV_DIM), f32),
            pltpu.VMEM((ATT_HEADS, 1), f32),
        ],
    )
    return pl.pallas_call(
        functools.partial(_paged_body, dec_seq=dec_seq),
        grid_spec=grid_spec,
        out_shape=jax.ShapeDtypeStruct((bsz, rows, KV_DIM), f32),
        compiler_params=_cparams(("parallel", "arbitrary")),
        name="paged_fox",
    )(page_table, qbd, knt, vnt, lfnt, *([ckt] * npg), *([cvt] * npg), *([clft] * npg))


def _merge_body(ys_ref, ya_ref, gs_ref, ga_ref, h_ref, ws_ref, wa_ref, wo_ref, o_ref):
    a = jax.nn.sigmoid(gs_ref[...]) * _dot(ys_ref[...], ws_ref[...])
    b = jax.nn.sigmoid(ga_ref[...]) * _dot(ya_ref[...], wa_ref[...])
    o_ref[...] = h_ref[...] + _dot((a + b).astype(bf16), wo_ref[...])


def _merge(y_ssd, y_att, main, h, ws, wa, wo, tm):
    n = h.shape[0]
    full = lambda shp: pl.BlockSpec(shp, lambda i: (0, 0))
    return pl.pallas_call(
        _merge_body,
        grid=(n // tm,),
        in_specs=[
            pl.BlockSpec((tm, D_SSM), lambda i: (i, 0)),
            pl.BlockSpec((tm, ATT_DIM), lambda i: (i, 0)),
            pl.BlockSpec((tm, D_MODEL), lambda i: (i, COL_GS // D_MODEL)),
            pl.BlockSpec((tm, D_MODEL), lambda i: (i, COL_GA // D_MODEL)),
            pl.BlockSpec((tm, D_MODEL), lambda i: (i, 0)),
            full((D_SSM, D_MODEL)), full((ATT_DIM, D_MODEL)), full((D_MODEL, D_MODEL)),
        ],
        out_specs=pl.BlockSpec((tm, D_MODEL), lambda i: (i, 0)),
        out_shape=jax.ShapeDtypeStruct((n, D_MODEL), f32),
        compiler_params=_cparams(("parallel",)),
        name="merge",
    )(y_ssd, y_att, main, main, h, ws, wa, wo)


def _pack_w_in(w_in):
    o = np.cumsum([0, D_SSM, CONV_DIM, SSD_HEADS, ATT_DIM, KV_DIM, KV_DIM, ATT_HEADS, D_MODEL, D_MODEL])
    z, xbc, dt, q, k, v, f, gs, ga = (w_in[:, int(o[i]):int(o[i + 1])] for i in range(9))
    pad = jnp.zeros((D_MODEL, COL_XBC - COL_KV - 2 * KV_DIM - SSD_HEADS - ATT_HEADS), w_in.dtype)
    w = jnp.concatenate([z, gs, ga, q, k, v, dt, f, pad, xbc], axis=1)
    return w.astype(bf16)


def _dense_pre(x2, p, tm):
    h = _ffn(x2, p["ffn1_norm"], p["ffn1_wg"], p["ffn1_wu"], p["ffn1_wd"], tm)
    main = _norm_matmul(h, p["mix_norm"], p["w_main"], tm, W_MAIN_COLS // 4)
    return h, main


def _dense_post(y_ssd, y_att, main, h, p, tm):
    h2 = _merge(y_ssd, y_att, main, h, p["w_ssd_proj"], p["w_attn_proj"], p["w_out"], tm)
    return _ffn(h2, p["ffn2_norm"], p["ffn2_wg"], p["ffn2_wu"], p["ffn2_wd"], tm)


def _ssd_args(p):
    return (p["conv_w"], p["conv_b"], p["dt_bias"], p["a_log"], p["d_skip_e"], p["ssd_norm"])


def _prompt_layer(x, p):
    bsz, seqlen, _ = x.shape
    n = bsz * seqlen
    tm = 512
    h, main = _dense_pre(x.reshape(n, D_MODEL), p, tm)
    main3 = main.reshape(bsz, seqlen, W_MAIN_COLS)
    qt, ka, vt, k, v, logf, dtp = _qkv_post(main3, p["q_norm_e"], p["k_norm_e"], p["b_f"], 512)
    conv0 = jnp.zeros((bsz, D_CONV - 1, CONV_DIM), f32)
    h0 = jnp.zeros((bsz, SSD_HEADS * SSD_HEAD_DIM, D_STATE), f32)
    y_ssd, h_new, conv_new = _ssd(main3, dtp, conv0, h0, *_ssd_args(p), SSD_CHUNK, SSD_CHUNK)
    y_att = _flash_fox(qt, ka, vt.reshape(bsz, KV_HEADS, ATT_HEAD_DIM, seqlen), 512)
    y = _dense_post(y_ssd.reshape(n, D_SSM), y_att.reshape(n, ATT_DIM), main, h, p, tm)
    return (y.reshape(bsz, seqlen, D_MODEL),
            k.reshape(bsz, seqlen, KV_HEADS, ATT_HEAD_DIM), v.reshape(bsz, seqlen, KV_HEADS, ATT_HEAD_DIM),
            logf, h_new.reshape(bsz, SSD_HEADS, SSD_HEAD_DIM, D_STATE), conv_new)


def _sample_layer(x, conv_prev, h0, ck, cv, clf, page_table, p):
    bsz, t, _ = x.shape
    n = bsz * t
    tm = n
    h, main = _dense_pre(x.reshape(n, D_MODEL), p, tm)
    qt, _, _, k, v, logf, dtp = _qkv_post(main.reshape(1, n, W_MAIN_COLS), p["q_norm_e"], p["k_norm_e"], p["b_f"], n)
    main3 = main.reshape(bsz, t, W_MAIN_COLS)
    y_ssd, h_new, conv_new = _ssd(main3, dtp.reshape(bsz, t, 128), conv_prev,
                                  h0.reshape(bsz, SSD_HEADS * SSD_HEAD_DIM, D_STATE), *_ssd_args(p), t, 16)
    qg = qt[0, :, 0:ATT_HEAD_DIM, :].reshape(KV_HEADS, Q_PER_KV, ATT_HEAD_DIM, bsz, t).transpose(3, 0, 1, 4, 2)
    eye = jnp.eye(KV_HEADS, dtype=bf16)
    qbd = (qg[:, :, :, :, None, :] * eye[None, :, None, None, :, None]).reshape(bsz, ATT_HEADS * t, KV_DIM)
    page = ck.shape[1]
    padt = lambda a: jnp.pad(a.reshape(bsz, t, -1).transpose(0, 2, 1), ((0, 0), (0, 0), (0, page - t)))
    o = _paged_fox(page_table, qbd, padt(k), padt(v), padt(logf),
                   ck.transpose(0, 2, 3, 1), cv.transpose(0, 2, 3, 1), clf.transpose(0, 2, 1), t)
    o6 = o.reshape(bsz, KV_HEADS, Q_PER_KV, t, KV_HEADS, ATT_HEAD_DIM)
    y_att = jnp.stack([o6[:, g, :, :, g, :] for g in range(KV_HEADS)], axis=1)
    y_att = y_att.transpose(0, 3, 1, 2, 4).reshape(n, ATT_DIM).astype(bf16)
    y = _dense_post(y_ssd.reshape(n, D_SSM), y_att, main, h, p, tm)
    return (y.reshape(bsz, t, D_MODEL),
            k.reshape(bsz, t, KV_HEADS, ATT_HEAD_DIM), v.reshape(bsz, t, KV_HEADS, ATT_HEAD_DIM),
            logf.reshape(bsz, t, ATT_HEADS), h_new.reshape(bsz, SSD_HEADS, SSD_HEAD_DIM, D_STATE), conv_new)


def kernel(x_prompt, x_sample, cache_k, cache_v, cache_logf, state_ssm, state_conv, page_table, ffn1_norm, ffn1_w_gate, ffn1_w_up, ffn1_w_down, mix_norm, w_in, conv_w, conv_b, dt_bias, a_log, d_skip, ssd_norm, q_norm, k_norm, b_f, w_ssd_proj, w_attn_proj, w_out, ffn2_norm, ffn2_w_gate, ffn2_w_up, ffn2_w_down):
    depth = w_in.shape[0]
    yp, ys = x_prompt, x_sample
    prompt_states, sample_states = [], []
    for l in range(depth):
        p = {
            "ffn1_norm": ffn1_norm[l][None], "ffn1_wg": ffn1_w_gate[l].astype(bf16),
            "ffn1_wu": ffn1_w_up[l].astype(bf16), "ffn1_wd": ffn1_w_down[l].astype(bf16),
            "mix_norm": mix_norm[l][None], "w_main": _pack_w_in(w_in[l]),
            "conv_w": conv_w[l], "conv_b": conv_b[l][None], "dt_bias": dt_bias[l][None], "a_log": a_log[l][None],
            "d_skip_e": jnp.repeat(d_skip[l], SSD_HEAD_DIM)[None], "ssd_norm": ssd_norm[l][None],
            "q_norm_e": (jnp.tile(q_norm[l], ATT_HEADS) * (ATT_HEAD_DIM ** -0.5))[None], "k_norm_e": jnp.tile(k_norm[l], KV_HEADS)[None],
            "b_f": b_f[l][None],
            "w_ssd_proj": w_ssd_proj[l].astype(bf16), "w_attn_proj": w_attn_proj[l].astype(bf16),
            "w_out": w_out[l].astype(bf16),
            "ffn2_norm": ffn2_norm[l][None], "ffn2_wg": ffn2_w_gate[l].astype(bf16),
            "ffn2_wu": ffn2_w_up[l].astype(bf16), "ffn2_wd": ffn2_w_down[l].astype(bf16),
        }
        yp, *sp = _prompt_layer(yp, p)
        ys, *ss = _sample_layer(ys, state_conv[l], state_ssm[l], cache_k[l], cache_v[l], cache_logf[l],
                                page_table, p)
        prompt_states.append(sp)
        sample_states.append(ss)
    k_p, v_p, logf_p, ssm_p, conv_p = (jnp.stack(t) for t in zip(*prompt_states))
    k_s, v_s, logf_s, ssm_s, conv_s = (jnp.stack(t) for t in zip(*sample_states))
    return (yp, ys, k_p, v_p, logf_p, ssm_p, conv_p, k_s, v_s, logf_s, ssm_s, conv_s)
```

```python
import functools
import numpy as np
import jax
import jax.numpy as jnp
from jax import lax
from jax.experimental import pallas as pl
from jax.experimental.pallas import tpu as pltpu

f32 = jnp.float32
bf16 = jnp.bfloat16

RMS_EPS = 1e-6
D_MODEL = 1024
D_FF = 2816
D_SSM = 2048
SSD_HEAD_DIM = 64
SSD_HEADS = 32
SSD_GROUPS = 4
HEADS_PER_GROUP = SSD_HEADS // SSD_GROUPS
D_STATE = 128
D_CONV = 4
CONV_DIM = D_SSM + 2 * SSD_GROUPS * D_STATE
SSD_CHUNK = 128
ATT_HEAD_DIM = 64
ATT_HEADS = 16
KV_HEADS = 4
Q_PER_KV = ATT_HEADS // KV_HEADS
ATT_DIM = ATT_HEADS * ATT_HEAD_DIM
KV_DIM = KV_HEADS * ATT_HEAD_DIM
NEG_BIG = -1e30

COL_Z = 0
COL_GS = 2048
COL_GA = 3072
COL_Q = 4096
COL_KV = 5120
COL_XBC = 6144
W_MAIN_COLS = 9216

VMEM_LIMIT = 56 * 1024 * 1024


def _cparams(sem):
    return pltpu.CompilerParams(dimension_semantics=sem, vmem_limit_bytes=VMEM_LIMIT)


def _split2(x):
    hi = x.astype(bf16)
    lo = (x - hi.astype(f32)).astype(bf16)
    return hi, lo


def _split3(x):
    hi = x.astype(bf16)
    r = x - hi.astype(f32)
    mid = r.astype(bf16)
    lo = (r - mid.astype(f32)).astype(bf16)
    return hi, mid, lo


def _dot(a, b):
    return jnp.dot(a, b, preferred_element_type=f32)


def _dot_nt(a, b):
    return lax.dot_general(a, b, (((1,), (1,)), ((), ())), preferred_element_type=f32)


def _dot_tn(a, b):
    return lax.dot_general(a, b, (((0,), (0,)), ((), ())), preferred_element_type=f32)


def _dot2_lhs(x, m):
    hi, lo = _split2(x)
    return _dot(hi, m) + _dot(lo, m)


FFN_CHUNK = 256


def _rms(x, w):
    return x * lax.rsqrt(jnp.mean(x * x, axis=-1, keepdims=True) + RMS_EPS) * w


def _ffn_body(x_ref, nw_ref, wg_ref, wu_ref, wd_ref, nw2_ref, o_ref, *rest, emit_norm):
    a_ref = rest[-1]
    xn = _rms(x_ref[...], nw_ref[...]).astype(bf16)
    for c in range(D_FF // FFN_CHUNK):
        cols = slice(c * FFN_CHUNK, (c + 1) * FFN_CHUNK)
        g = _dot(xn, wg_ref[:, cols])
        u = _dot(xn, wu_ref[:, cols])
        a_ref[:, cols] = ((g * jax.nn.sigmoid(g)) * u).astype(bf16)
    y = x_ref[...] + 0.5 * _dot(a_ref[...], wd_ref[...])
    o_ref[...] = y
    if emit_norm:
        rest[0][...] = _rms(y, nw2_ref[...]).astype(bf16)


def _ffn(x, nw, wg, wu, wd, tm, nw2=None):
    n = x.shape[0]
    emit_norm = nw2 is not None
    full = lambda shp: pl.BlockSpec(shp, lambda i: (0, 0))
    row = pl.BlockSpec((tm, D_MODEL), lambda i: (i, 0))
    out_shape = [jax.ShapeDtypeStruct((n, D_MODEL), f32)]
    if emit_norm:
        out_shape.append(jax.ShapeDtypeStruct((n, D_MODEL), bf16))
    outs = pl.pallas_call(
        functools.partial(_ffn_body, emit_norm=emit_norm),
        grid=(n // tm,),
        in_specs=[row, full((1, D_MODEL)), full((D_MODEL, D_FF)), full((D_MODEL, D_FF)), full((D_FF, D_MODEL)),
                  full((1, D_MODEL))],
        out_specs=[row] * len(out_shape),
        out_shape=out_shape,
        scratch_shapes=[pltpu.VMEM((tm, D_FF), bf16)],
        compiler_params=_cparams(("parallel",)),
        name="ffn",
    )(x, nw, wg, wu, wd, nw2 if emit_norm else nw)
    return outs if emit_norm else outs[0]


def _proj_body(x_ref, w_ref, o_ref):
    o_ref[...] = _dot(x_ref[...], w_ref[...])


def _in_proj(xn, w, tm, tn):
    n = xn.shape[0]
    c = w.shape[1]
    return pl.pallas_call(
        _proj_body,
        grid=(c // tn, n // tm),
        in_specs=[
            pl.BlockSpec((tm, D_MODEL), lambda j, i: (i, 0)),
            pl.BlockSpec((D_MODEL, tn), lambda j, i: (0, j)),
        ],
        out_specs=pl.BlockSpec((tm, tn), lambda j, i: (i, j)),
        out_shape=jax.ShapeDtypeStruct((n, c), f32),
        compiler_params=_cparams(("parallel", "parallel")),
        name="in_proj",
    )(xn, w)


AUG_DIM = 128
BIAS_PIECES = 3


def _qkv_body(q_ref, kv_ref, qw_ref, kw_ref, bf_ref, seg_ref, segt_ref, tril_ref, pk_ref, pc_ref,
              qt_ref, ka_ref, vt_ref, ko_ref, vo_ref, lf_ref, dtp_ref, carry_ref):
    @pl.when(pl.program_id(1) == 0)
    def _():
        carry_ref[...] = jnp.zeros_like(carry_ref)

    seg = seg_ref[...]
    segt = segt_ref[...]
    tl = q_ref.shape[0]

    q = q_ref[...]
    msq = _dot2_lhs(q * q, seg)
    fac = _dot2_lhs(lax.rsqrt(msq + RMS_EPS), segt)
    qnt = (q * fac * qw_ref[...]).T
    row = lax.broadcasted_iota(jnp.int32, (AUG_DIM - ATT_HEAD_DIM, tl), 0)
    for h in range(ATT_HEADS):
        qt_ref[h, 0:ATT_HEAD_DIM, :] = qnt[h * ATT_HEAD_DIM:(h + 1) * ATT_HEAD_DIM, :].astype(bf16)
        lo = BIAS_PIECES * (h % Q_PER_KV)
        qt_ref[h, ATT_HEAD_DIM:AUG_DIM, :] = jnp.where((row >= lo) & (row < lo + BIAS_PIECES), 1.0, 0.0).astype(bf16)

    kv = kv_ref[...]
    k = kv[:, 0:KV_DIM]
    msk = _dot2_lhs(k * k, seg[0:KV_DIM, :])
    fack = _dot2_lhs(lax.rsqrt(msk + RMS_EPS), segt[:, 0:KV_DIM])
    kn = k * fack * kw_ref[...]
    ko_ref[...] = kn
    v = kv[:, KV_DIM:2 * KV_DIM]
    vo_ref[...] = v
    vt_ref[...] = v.T.astype(bf16)
    dtp_ref[...] = kv[:, 2 * KV_DIM:2 * KV_DIM + 128]
    f_raw = kv[:, 2 * KV_DIM + SSD_HEADS:2 * KV_DIM + SSD_HEADS + ATT_HEADS]
    lf = jax.nn.log_sigmoid(f_raw + bf_ref[...])
    lf_ref[...] = lf

    tril = tril_ref[...]
    p0, p1, p2 = _split3(lf)
    c = _dot(tril, p0) + _dot(tril, p1) + _dot(tril, p2) + carry_ref[...]
    carry_ref[...] = c[tl - 1:tl, :]
    n0, n1, n2 = _split3(-c)
    ka = _dot(kn.astype(bf16), pk_ref[...]) + _dot(n0, pc_ref[0]) + _dot(n1, pc_ref[1]) + _dot(n2, pc_ref[2])
    for g in range(KV_HEADS):
        ka_ref[g] = ka[:, g * AUG_DIM:(g + 1) * AUG_DIM].astype(bf16)


def _qkv_post(main3, qw, kw, b_f, tl):
    bsz, seqlen, _ = main3.shape
    seg = np.zeros((ATT_DIM, 128), np.float32)
    for h in range(ATT_HEADS):
        seg[h * ATT_HEAD_DIM:(h + 1) * ATT_HEAD_DIM, h] = 1.0 / ATT_HEAD_DIM
    segt = (seg.T > 0).astype(np.float32)
    tril = np.tril(np.ones((tl, tl), np.float32))
    pk = np.zeros((KV_DIM, KV_HEADS * AUG_DIM), np.float32)
    pc = np.zeros((BIAS_PIECES, ATT_HEADS, KV_HEADS * AUG_DIM), np.float32)
    for g in range(KV_HEADS):
        for d in range(ATT_HEAD_DIM):
            pk[g * ATT_HEAD_DIM + d, g * AUG_DIM + d] = 1.0
        for j in range(Q_PER_KV):
            for i in range(BIAS_PIECES):
                pc[i, g * Q_PER_KV + j, g * AUG_DIM + ATT_HEAD_DIM + BIAS_PIECES * j + i] = 1.0
    full = lambda shp: pl.BlockSpec(shp, lambda b, i: tuple(0 for _ in shp))
    outs = pl.pallas_call(
        _qkv_body,
        grid=(bsz, seqlen // tl),
        in_specs=[
            pl.BlockSpec((None, tl, 1024), lambda b, i: (b, i, COL_Q // 1024)),
            pl.BlockSpec((None, tl, 1024), lambda b, i: (b, i, COL_KV // 1024)),
            full((1, ATT_DIM)), full((1, KV_DIM)), full((1, ATT_HEADS)),
            full((ATT_DIM, 128)), full((128, ATT_DIM)), full((tl, tl)),
            full((KV_DIM, KV_HEADS * AUG_DIM)), full((BIAS_PIECES, ATT_HEADS, KV_HEADS * AUG_DIM)),
        ],
        out_specs=[
            pl.BlockSpec((None, ATT_HEADS, AUG_DIM, tl), lambda b, i: (b, 0, 0, i)),
            pl.BlockSpec((None, KV_HEADS, tl, AUG_DIM), lambda b, i: (b, 0, i, 0)),
            pl.BlockSpec((None, KV_DIM, tl), lambda b, i: (b, 0, i)),
            pl.BlockSpec((None, tl, KV_DIM), lambda b, i: (b, i, 0)),
            pl.BlockSpec((None, tl, KV_DIM), lambda b, i: (b, i, 0)),
            pl.BlockSpec((None, tl, ATT_HEADS), lambda b, i: (b, i, 0)),
            pl.BlockSpec((None, tl, 128), lambda b, i: (b, i, 0)),
        ],
        out_shape=[
            jax.ShapeDtypeStruct((bsz, ATT_HEADS, AUG_DIM, seqlen), bf16),
            jax.ShapeDtypeStruct((bsz, KV_HEADS, seqlen, AUG_DIM), bf16),
            jax.ShapeDtypeStruct((bsz, KV_DIM, seqlen), bf16),
            jax.ShapeDtypeStruct((bsz, seqlen, KV_DIM), f32),
            jax.ShapeDtypeStruct((bsz, seqlen, KV_DIM), f32),
            jax.ShapeDtypeStruct((bsz, seqlen, ATT_HEADS), f32),
            jax.ShapeDtypeStruct((bsz, seqlen, 128), f32),
        ],
        scratch_shapes=[pltpu.VMEM((1, ATT_HEADS), f32)],
        compiler_params=_cparams(("parallel", "arbitrary")),
        name="qkv_post",
    )(main3, main3, qw, kw, b_f, jnp.asarray(seg, bf16), jnp.asarray(segt, bf16), jnp.asarray(tril, bf16),
      jnp.asarray(pk, bf16), jnp.asarray(pc, bf16))
    return outs


def _ssd_body(xbc_ref, z_ref, dtp_ref, prev_ref, h0_ref, cw_ref, cb_ref, dtb_ref, alog_ref, dskip_ref,
              nrm_ref, e2_ref, tril_ref, triu_ref, y_ref, hnew_ref, cnew_ref, xpad_ref, hs_ref, *, qr, q):
    c = pl.program_id(1)
    last = pl.num_programs(1) - 1

    @pl.when(c == 0)
    def _():
        xpad_ref[0:8, :] = jnp.zeros((8, CONV_DIM), f32)
        xpad_ref[8 - (D_CONV - 1):8, :] = prev_ref[...]
        hs_ref[...] = h0_ref[...]

    x = xbc_ref[...]
    xpad_ref[8:8 + qr, :] = x
    xa = xpad_ref[...]
    conv = cb_ref[...] + cw_ref[D_CONV - 1:D_CONV, :] * x
    for j in range(D_CONV - 1):
        shifted = pltpu.roll(xa, D_CONV - 1 - j, axis=0)[8:8 + qr, :]
        conv = conv + cw_ref[j:j + 1, :] * shifted
    xc = conv * jax.nn.sigmoid(conv)

    @pl.when(c == last)
    def _():
        cnew_ref[...] = xpad_ref[qr + 5:qr + 8, :]

    xpad_ref[0:8, :] = xpad_ref[qr:qr + 8, :]

    dt = jax.nn.softplus(dtp_ref[...][:, 0:SSD_HEADS] + dtb_ref[...])
    zz = z_ref[...]
    if q != qr:
        xc = jnp.concatenate([xc, jnp.zeros((q - qr, CONV_DIM), f32)], axis=0)
        dt = jnp.concatenate([dt, jnp.zeros((q - qr, SSD_HEADS), f32)], axis=0)
        zz = jnp.concatenate([zz, jnp.zeros((q - qr, D_SSM), f32)], axis=0)
    xs = xc[:, 0:D_SSM]
    bm = xc[:, D_SSM:D_SSM + SSD_GROUPS * D_STATE].astype(bf16)
    cm = xc[:, D_SSM + SSD_GROUPS * D_STATE:CONV_DIM].astype(bf16)

    a = dt * (-jnp.exp(alog_ref[...]))
    tril = tril_ref[...]
    a0, a1, a2 = _split3(a)
    acs = _dot(tril, a0) + _dot(tril, a1) + _dot(tril, a2)
    triu = triu_ref[...]
    acst = _dot_tn(a0, triu) + _dot_tn(a1, triu) + _dot_tn(a2, triu)
    acs_last = acs[q - 1:q, :]
    fac = jnp.concatenate([dt, jnp.exp(acs_last - acs), jnp.exp(acs)], axis=0)
    fac_e = _dot(jnp.concatenate(_split2(fac), axis=1), e2_ref[...])
    xdt = xs * fac_e[0:q]
    xdtw = (xdt * fac_e[q:2 * q]).astype(bf16)
    eacs_e = fac_e[2 * q:3 * q]
    edec = jnp.exp(acst[:, q - 1:q])

    li = lax.broadcasted_iota(jnp.int32, (q, q), 0)
    si = lax.broadcasted_iota(jnp.int32, (q, q), 1)
    causal = li >= si
    lane = lax.broadcasted_iota(jnp.int32, (q, 2 * SSD_HEAD_DIM), 1)
    left = lane < SSD_HEAD_DIM

    y_groups = []
    for g in range(SSD_GROUPS):
        bg = bm[:, g * D_STATE:(g + 1) * D_STATE]
        cg = cm[:, g * D_STATE:(g + 1) * D_STATE]
        cbm = _dot_nt(cg, bg)
        r0 = g * HEADS_PER_GROUP * SSD_HEAD_DIM
        hg = hs_ref[r0:r0 + HEADS_PER_GROUP * SSD_HEAD_DIM, :].astype(bf16)
        y_off = _dot_nt(cg, hg)
        st = _dot_tn(xdtw[:, r0:r0 + HEADS_PER_GROUP * SSD_HEAD_DIM], bg)
        for hh in range(HEADS_PER_GROUP):
            h = g * HEADS_PER_GROUP + hh
            rows = slice(h * SSD_HEAD_DIM, (h + 1) * SSD_HEAD_DIM)
            hs_ref[rows, :] = (hs_ref[rows, :] * edec[h:h + 1, :]
                               + st[hh * SSD_HEAD_DIM:(hh + 1) * SSD_HEAD_DIM, :])
        pairs = []
        for pr in range(HEADS_PER_GROUP // 2):
            h0i = g * HEADS_PER_GROUP + 2 * pr
            xpair = xdt[:, h0i * SSD_HEAD_DIM:(h0i + 2) * SSD_HEAD_DIM]
            acc = None
            for t in range(2):
                h = h0i + t
                sg = acs[:, h:h + 1] - acst[h:h + 1, :]
                m = (cbm * jnp.exp(jnp.where(causal, sg, NEG_BIG))).astype(bf16)
                xm = jnp.where(left if t == 0 else jnp.logical_not(left), xpair, 0.0).astype(bf16)
                d = _dot(m, xm)
                acc = d if acc is None else acc + d
            pairs.append(acc)
        y_diag = jnp.concatenate(pairs, axis=1)
        y_groups.append(y_diag + y_off * eacs_e[:, r0:r0 + HEADS_PER_GROUP * SSD_HEAD_DIM])
    y = jnp.concatenate(y_groups, axis=1) + dskip_ref[...] * xs

    @pl.when(c == last)
    def _():
        hnew_ref[...] = hs_ref[...]

    yg = y * (zz * jax.nn.sigmoid(zz))
    gw = D_SSM // SSD_GROUPS
    outs = []
    for g in range(SSD_GROUPS):
        t = yg[:, g * gw:(g + 1) * gw]
        outs.append(t * lax.rsqrt(jnp.mean(t * t, axis=-1, keepdims=True) + RMS_EPS))
    yn = jnp.concatenate(outs, axis=1) * nrm_ref[...]
    y_ref[...] = yn[0:qr, :].astype(bf16)


def _ssd(main3, dtp, conv_prev, h0, conv_w, conv_b, dt_bias, a_log, d_skip_e, ssd_norm, qr, q):
    bsz, seqlen, _ = main3.shape
    nc = seqlen // qr
    e = np.zeros((SSD_HEADS, D_SSM), np.float32)
    for h in range(SSD_HEADS):
        e[h, h * SSD_HEAD_DIM:(h + 1) * SSD_HEAD_DIM] = 1.0
    tril = np.tril(np.ones((q, q), np.float32))
    nst = SSD_HEADS * SSD_HEAD_DIM
    full = lambda shp: pl.BlockSpec(shp, lambda b, c: tuple(0 for _ in shp))
    y, hnew, cnew = pl.pallas_call(
        functools.partial(_ssd_body, qr=qr, q=q),
        grid=(bsz, nc),
        in_specs=[
            pl.BlockSpec((None, qr, CONV_DIM), lambda b, c: (b, c, COL_XBC // CONV_DIM)),
            pl.BlockSpec((None, qr, D_SSM), lambda b, c: (b, c, COL_Z // D_SSM)),
            pl.BlockSpec((None, qr, 128), lambda b, c: (b, c, 0)),
            pl.BlockSpec((None, D_CONV - 1, CONV_DIM), lambda b, c: (b, 0, 0)),
            pl.BlockSpec((None, nst, D_STATE), lambda b, c: (b, 0, 0)),
            full((D_CONV, CONV_DIM)), full((1, CONV_DIM)), full((1, SSD_HEADS)), full((1, SSD_HEADS)),
            full((1, D_SSM)), full((1, D_SSM)), full((2 * SSD_HEADS, D_SSM)),
            full((q, q)), full((q, q)),
        ],
        out_specs=[
            pl.BlockSpec((None, qr, D_SSM), lambda b, c: (b, c, 0)),
            pl.BlockSpec((None, nst, D_STATE), lambda b, c: (b, 0, 0)),
            pl.BlockSpec((None, D_CONV - 1, CONV_DIM), lambda b, c: (b, 0, 0)),
        ],
        out_shape=[
            jax.ShapeDtypeStruct((bsz, seqlen, D_SSM), bf16),
            jax.ShapeDtypeStruct((bsz, nst, D_STATE), f32),
            jax.ShapeDtypeStruct((bsz, D_CONV - 1, CONV_DIM), f32),
        ],
        scratch_shapes=[pltpu.VMEM((qr + 8, CONV_DIM), f32), pltpu.VMEM((nst, D_STATE), f32)],
        compiler_params=_cparams(("parallel", "arbitrary")),
        name="ssd",
    )(main3, main3, dtp, conv_prev, h0, conv_w, conv_b, dt_bias, a_log, d_skip_e, ssd_norm,
      jnp.asarray(np.concatenate([e, e], axis=0), bf16), jnp.asarray(tril, bf16), jnp.asarray(tril.T, bf16))
    return y, hnew, cnew


FLASH_Q_LANES = 256
FLASH_KEY_ROWS = 128
FLASH_LOOKAHEAD = 4


def _flash_body(qi_ref, ki_ref, qt_ref, ka_ref, vt_ref, o_ref, m_ref, l_ref, acc_ref, *, tb):
    t = pl.program_id(2)
    qi = qi_ref[t]
    ki = ki_ref[t]
    ql, kr = FLASH_Q_LANES, FLASH_KEY_ROWS

    @pl.when(ki == 0)
    def _():
        m_ref[...] = jnp.full_like(m_ref, NEG_BIG)
        l_ref[...] = jnp.zeros_like(l_ref)
        acc_ref[...] = jnp.zeros_like(acc_ref)

    def process(diagonal):
        key_i = lax.broadcasted_iota(jnp.int32, (kr, ql), 0)
        qry_i = lax.broadcasted_iota(jnp.int32, (kr, ql), 1)
        tiles = [(j, r, kb) for kb in range(tb // kr) for j in range(Q_PER_KV) for r in range(tb // ql)
                 if not (diagonal and kb * kr > r * ql + ql - 1)]

        def scores(tile):
            j, r, kb = tile
            return _dot(ka_ref[kb * kr:(kb + 1) * kr, :], qt_ref[j, :, r * ql:(r + 1) * ql])

        def finish(tile, st):
            j, r, kb = tile
            cols = slice(r * ql, (r + 1) * ql)
            if diagonal and kb * kr + kr - 1 > r * ql:
                st = jnp.where(key_i + kb * kr <= qry_i + r * ql, st, NEG_BIG)
            m_prev = m_ref[j, :, cols]
            m_new = jnp.maximum(m_prev, jnp.max(st, axis=0, keepdims=True))
            alpha = jnp.exp(m_prev - m_new)
            p = jnp.exp(st - m_new)
            l_ref[j, :, cols] = alpha * l_ref[j, :, cols] + jnp.sum(p, axis=0, keepdims=True)
            acc_ref[j, :, cols] = (alpha * acc_ref[j, :, cols]
                                   + _dot(vt_ref[:, kb * kr:(kb + 1) * kr], p.astype(bf16)))
            m_ref[j, :, cols] = m_new

        pending = [scores(tl_) for tl_ in tiles[:FLASH_LOOKAHEAD]]
        for i, tile in enumerate(tiles):
            if i + FLASH_LOOKAHEAD < len(tiles):
                pending.append(scores(tiles[i + FLASH_LOOKAHEAD]))
            finish(tile, pending.pop(0))

    pl.when(ki < qi)(lambda: process(False))

    @pl.when(ki == qi)
    def _():
        process(True)
        o = acc_ref[...] / l_ref[...]
        o_ref[...] = jnp.concatenate([o[j].T for j in range(Q_PER_KV)], axis=1).astype(bf16)


def _flash_fox(qt, ka, vt, tb):
    bsz, _, _, seqlen = qt.shape
    nb = seqlen // tb
    qi_tab = np.array([i for i in range(nb) for j in range(i + 1)], np.int32)
    ki_tab = np.array([j for i in range(nb) for j in range(i + 1)], np.int32)
    grid_spec = pltpu.PrefetchScalarGridSpec(
        num_scalar_prefetch=2,
        grid=(bsz, KV_HEADS, qi_tab.shape[0]),
        in_specs=[
            pl.BlockSpec((None, Q_PER_KV, AUG_DIM, tb), lambda b, g, t, qi, ki: (b, g, 0, qi[t])),
            pl.BlockSpec((None, None, tb, AUG_DIM), lambda b, g, t, qi, ki: (b, g, ki[t], 0)),
            pl.BlockSpec((None, None, ATT_HEAD_DIM, tb), lambda b, g, t, qi, ki: (b, g, 0, ki[t])),
        ],
        out_specs=pl.BlockSpec((None, tb, Q_PER_KV * ATT_HEAD_DIM), lambda b, g, t, qi, ki: (b, qi[t], g)),
        scratch_shapes=[
            pltpu.VMEM((Q_PER_KV, 1, tb), f32),
            pltpu.VMEM((Q_PER_KV, 1, tb), f32),
            pltpu.VMEM((Q_PER_KV, ATT_HEAD_DIM, tb), f32),
        ],
    )
    return pl.pallas_call(
        functools.partial(_flash_body, tb=tb),
        grid_spec=grid_spec,
        out_shape=jax.ShapeDtypeStruct((bsz, seqlen, ATT_DIM), bf16),
        compiler_params=_cparams(("parallel", "parallel", "arbitrary")),
        name="flash_fox",
    )(jnp.asarray(qi_tab), jnp.asarray(ki_tab), qt, ka, vt)


PAGES_PER_STEP = 16
PAGES_PER_SUBBLOCK = 16


def _lane_cumsum(x):
    n = x.shape[1]
    lane = lax.broadcasted_iota(jnp.int32, x.shape, 1)
    sh = 1
    while sh < n:
        x = x + jnp.where(lane >= sh, pltpu.roll(x, sh, axis=1), 0.0)
        sh *= 2
    return x


def _paged_body(pt_ref, qbd_ref, kn_ref, vn_ref, lfn_ref, *rest, dec_seq):
    npg = PAGES_PER_STEP
    sub = PAGES_PER_SUBBLOCK
    k_refs = rest[0:npg]
    v_refs = rest[npg:2 * npg]
    lf_refs = rest[2 * npg:3 * npg]
    o_ref, m_ref, l_ref, acc_ref, carry_ref = rest[3 * npg:]
    j = pl.program_id(1)
    rows = qbd_ref.shape[0]

    @pl.when(j == 0)
    def _():
        m_ref[...] = jnp.full_like(m_ref, NEG_BIG)
        l_ref[...] = jnp.zeros_like(l_ref)
        acc_ref[...] = jnp.zeros_like(acc_ref)
        carry_ref[...] = jnp.zeros_like(carry_ref)

    qbd = qbd_ref[...]

    def attend(kt, vt, lft, mask):
        nkeys = kt.shape[1]
        ct = _lane_cumsum(lft) + carry_ref[...]
        carry_ref[...] = ct[:, nkeys - 1:nkeys]
        s = _dot(qbd, kt.astype(bf16))
        s3 = s.reshape(ATT_HEADS, rows // ATT_HEADS, nkeys) - ct[:, None, :]
        s = s3.reshape(rows, nkeys)
        if mask is not None:
            s = jnp.where(mask, s, NEG_BIG)
        m_prev = m_ref[...]
        m_new = jnp.maximum(m_prev, jnp.max(s, axis=-1, keepdims=True))
        alpha = jnp.exp(m_prev - m_new)
        p = jnp.exp(s - m_new)
        l_ref[...] = alpha * l_ref[...] + jnp.sum(p, axis=-1, keepdims=True)
        acc_ref[...] = alpha * acc_ref[...] + _dot_nt(p.astype(bf16), vt.astype(bf16))
        m_ref[...] = m_new

    for sb in range(npg // sub):
        pages = range(sb * sub, (sb + 1) * sub)
        kt = jnp.concatenate([k_refs[i][...].reshape(KV_DIM, -1) for i in pages], axis=1)
        vt = jnp.concatenate([v_refs[i][...].reshape(KV_DIM, -1) for i in pages], axis=1)
        lft = jnp.concatenate([lf_refs[i][...] for i in pages], axis=1)
        attend(kt, vt, lft, None)

    @pl.when(j == pl.num_programs(1) - 1)
    def _():
        nk = kn_ref.shape[1]
        r = lax.broadcasted_iota(jnp.int32, (rows, nk), 0) % dec_seq
        t = lax.broadcasted_iota(jnp.int32, (rows, nk), 1)
        attend(kn_ref[...], vn_ref[...], lfn_ref[...], t <= r)
        o_ref[...] = acc_ref[...] / l_ref[...]


def _paged_fox(page_table, qbd, knt, vnt, lfnt, ckt, cvt, clft, dec_seq):
    bsz, n_pages = page_table.shape
    rows = qbd.shape[1]
    page = ckt.shape[3]
    npg = PAGES_PER_STEP
    pmap4 = lambda i: (lambda b, j, pt: (pt[b, j * npg + i], 0, 0, 0))
    pmap3 = lambda i: (lambda b, j, pt: (pt[b, j * npg + i], 0, 0))
    bmap = lambda b, j, pt: (b, 0, 0)
    in_specs = [
        pl.BlockSpec((None, rows, KV_DIM), bmap),
        pl.BlockSpec((None, KV_DIM, knt.shape[2]), bmap),
        pl.BlockSpec((None, KV_DIM, vnt.shape[2]), bmap),
        pl.BlockSpec((None, ATT_HEADS, lfnt.shape[2]), bmap),
    ]
    in_specs += [pl.BlockSpec((None, KV_HEADS, ATT_HEAD_DIM, page), pmap4(i)) for i in range(npg)]
    in_specs += [pl.BlockSpec((None, KV_HEADS, ATT_HEAD_DIM, page), pmap4(i)) for i in range(npg)]
    in_specs += [pl.BlockSpec((None, ATT_HEADS, page), pmap3(i)) for i in range(npg)]
    grid_spec = pltpu.PrefetchScalarGridSpec(
        num_scalar_prefetch=1,
        grid=(bsz, n_pages // npg),
        in_specs=in_specs,
        out_specs=pl.BlockSpec((None, rows, KV_DIM), bmap),
        scratch_shapes=[
            pltpu.VMEM((rows, 1), f32), pltpu.VMEM((rows, 1), f32), pltpu.VMEM((rows, KV_DIM), f32),
            pltpu.VMEM((ATT_HEADS, 1), f32),
        ],
    )
    return pl.pallas_call(
        functools.partial(_paged_body, dec_seq=dec_seq),
        grid_spec=grid_spec,
        out_shape=jax.ShapeDtypeStruct((bsz, rows, KV_DIM), f32),
        compiler_params=_cparams(("parallel", "arbitrary")),
        name="paged_fox",
    )(page_table, qbd, knt, vnt, lfnt, *([ckt] * npg), *([cvt] * npg), *([clft] * npg))


def _merge_body(ys_ref, ya_ref, gs_ref, ga_ref, h_ref, ws_ref, wa_ref, wo_ref, o_ref):
    a = jax.nn.sigmoid(gs_ref[...]) * _dot(ys_ref[...], ws_ref[...])
    b = jax.nn.sigmoid(ga_ref[...]) * _dot(ya_ref[...], wa_ref[...])
    o_ref[...] = h_ref[...] + _dot((a + b).astype(bf16), wo_ref[...])


def _merge(y_ssd, y_att, main, h, ws, wa, wo, tm):
    n = h.shape[0]
    full = lambda shp: pl.BlockSpec(shp, lambda i: (0, 0))
    return pl.pallas_call(
        _merge_body,
        grid=(n // tm,),
        in_specs=[
            pl.BlockSpec((tm, D_SSM), lambda i: (i, 0)),
            pl.BlockSpec((tm, ATT_DIM), lambda i: (i, 0)),
            pl.BlockSpec((tm, D_MODEL), lambda i: (i, COL_GS // D_MODEL)),
            pl.BlockSpec((tm, D_MODEL), lambda i: (i, COL_GA // D_MODEL)),
            pl.BlockSpec((tm, D_MODEL), lambda i: (i, 0)),
            full((D_SSM, D_MODEL)), full((ATT_DIM, D_MODEL)), full((D_MODEL, D_MODEL)),
        ],
        out_specs=pl.BlockSpec((tm, D_MODEL), lambda i: (i, 0)),
        out_shape=jax.ShapeDtypeStruct((n, D_MODEL), f32),
        compiler_params=_cparams(("parallel",)),
        name="merge",
    )(y_ssd, y_att, main, main, h, ws, wa, wo)


def _pack_w_in(w_in):
    o = np.cumsum([0, D_SSM, CONV_DIM, SSD_HEADS, ATT_DIM, KV_DIM, KV_DIM, ATT_HEADS, D_MODEL, D_MODEL])
    z, xbc, dt, q, k, v, f, gs, ga = (w_in[:, int(o[i]):int(o[i + 1])] for i in range(9))
    pad = jnp.zeros((D_MODEL, COL_XBC - COL_KV - 2 * KV_DIM - SSD_HEADS - ATT_HEADS), w_in.dtype)
    w = jnp.concatenate([z, gs, ga, q, k, v, dt, f, pad, xbc], axis=1)
    return w.astype(bf16)


def _dense_pre(x2, p, tm):
    h, hn = _ffn(x2, p["ffn1_norm"], p["ffn1_wg"], p["ffn1_wu"], p["ffn1_wd"], tm, p["mix_norm"])
    main = _in_proj(hn, p["w_main"], min(x2.shape[0], 1024), W_MAIN_COLS // 4)
    return h, main


def _dense_post(y_ssd, y_att, main, h, p, tm):
    h2 = _merge(y_ssd, y_att, main, h, p["w_ssd_proj"], p["w_attn_proj"], p["w_out"], tm)
    return _ffn(h2, p["ffn2_norm"], p["ffn2_wg"], p["ffn2_wu"], p["ffn2_wd"], tm)


def _ssd_args(p):
    return (p["conv_w"], p["conv_b"], p["dt_bias"], p["a_log"], p["d_skip_e"], p["ssd_norm"])


def _prompt_layer(x, p):
    bsz, seqlen, _ = x.shape
    n = bsz * seqlen
    tm = 512
    h, main = _dense_pre(x.reshape(n, D_MODEL), p, tm)
    main3 = main.reshape(bsz, seqlen, W_MAIN_COLS)
    qt, ka, vt, k, v, logf, dtp = _qkv_post(main3, p["q_norm_e"], p["k_norm_e"], p["b_f"], 512)
    conv0 = jnp.zeros((bsz, D_CONV - 1, CONV_DIM), f32)
    h0 = jnp.zeros((bsz, SSD_HEADS * SSD_HEAD_DIM, D_STATE), f32)
    y_ssd, h_new, conv_new = _ssd(main3, dtp, conv0, h0, *_ssd_args(p), SSD_CHUNK, SSD_CHUNK)
    y_att = _flash_fox(qt, ka, vt.reshape(bsz, KV_HEADS, ATT_HEAD_DIM, seqlen), 512)
    y = _dense_post(y_ssd.reshape(n, D_SSM), y_att.reshape(n, ATT_DIM), main, h, p, tm)
    return (y.reshape(bsz, seqlen, D_MODEL),
            k.reshape(bsz, seqlen, KV_HEADS, ATT_HEAD_DIM), v.reshape(bsz, seqlen, KV_HEADS, ATT_HEAD_DIM),
            logf, h_new.reshape(bsz, SSD_HEADS, SSD_HEAD_DIM, D_STATE), conv_new)


def _sample_layer(x, conv_prev, h0, ck, cv, clf, page_table, p):
    bsz, t, _ = x.shape
    n = bsz * t
    tm = n
    h, main = _dense_pre(x.reshape(n, D_MODEL), p, tm)
    qt, _, _, k, v, logf, dtp = _qkv_post(main.reshape(1, n, W_MAIN_COLS), p["q_norm_e"], p["k_norm_e"], p["b_f"], n)
    main3 = main.reshape(bsz, t, W_MAIN_COLS)
    y_ssd, h_new, conv_new = _ssd(main3, dtp.reshape(bsz, t, 128), conv_prev,
                                  h0.reshape(bsz, SSD_HEADS * SSD_HEAD_DIM, D_STATE), *_ssd_args(p), t, 16)
    qg = qt[0, :, 0:ATT_HEAD_DIM, :].reshape(KV_HEADS, Q_PER_KV, ATT_HEAD_DIM, bsz, t).transpose(3, 0, 1, 4, 2)
    eye = jnp.eye(KV_HEADS, dtype=bf16)
    qbd = (qg[:, :, :, :, None, :] * eye[None, :, None, None, :, None]).reshape(bsz, ATT_HEADS * t, KV_DIM)
    page = ck.shape[1]
    padt = lambda a: jnp.pad(a.reshape(bsz, t, -1).transpose(0, 2, 1), ((0, 0), (0, 0), (0, page - t)))
    o = _paged_fox(page_table, qbd, padt(k), padt(v), padt(logf),
                   ck.transpose(0, 2, 3, 1), cv.transpose(0, 2, 3, 1), clf.transpose(0, 2, 1), t)
    o6 = o.reshape(bsz, KV_HEADS, Q_PER_KV, t, KV_HEADS, ATT_HEAD_DIM)
    y_att = jnp.stack([o6[:, g, :, :, g, :] for g in range(KV_HEADS)], axis=1)
    y_att = y_att.transpose(0, 3, 1, 2, 4).reshape(n, ATT_DIM).astype(bf16)
    y = _dense_post(y_ssd.reshape(n, D_SSM), y_att, main, h, p, tm)
    return (y.reshape(bsz, t, D_MODEL),
            k.reshape(bsz, t, KV_HEADS, ATT_HEAD_DIM), v.reshape(bsz, t, KV_HEADS, ATT_HEAD_DIM),
            logf.reshape(bsz, t, ATT_HEADS), h_new.reshape(bsz, SSD_HEADS, SSD_HEAD_DIM, D_STATE), conv_new)


def kernel(x_prompt, x_sample, cache_k, cache_v, cache_logf, state_ssm, state_conv, page_table, ffn1_norm, ffn1_w_gate, ffn1_w_up, ffn1_w_down, mix_norm, w_in, conv_w, conv_b, dt_bias, a_log, d_skip, ssd_norm, q_norm, k_norm, b_f, w_ssd_proj, w_attn_proj, w_out, ffn2_norm, ffn2_w_gate, ffn2_w_up, ffn2_w_down):
    depth = w_in.shape[0]
    yp, ys = x_prompt, x_sample
    prompt_states, sample_states = [], []
    for l in range(depth):
        p = {
            "ffn1_norm": ffn1_norm[l][None], "ffn1_wg": ffn1_w_gate[l].astype(bf16),
            "ffn1_wu": ffn1_w_up[l].astype(bf16), "ffn1_wd": ffn1_w_down[l].astype(bf16),
            "mix_norm": mix_norm[l][None], "w_main": _pack_w_in(w_in[l]),
            "conv_w": conv_w[l], "conv_b": conv_b[l][None], "dt_bias": dt_bias[l][None], "a_log": a_log[l][None],
            "d_skip_e": jnp.repeat(d_skip[l], SSD_HEAD_DIM)[None], "ssd_norm": ssd_norm[l][None],
            "q_norm_e": (jnp.tile(q_norm[l], ATT_HEADS) * (ATT_HEAD_DIM ** -0.5))[None], "k_norm_e": jnp.tile(k_norm[l], KV_HEADS)[None],
            "b_f": b_f[l][None],
            "w_ssd_proj": w_ssd_proj[l].astype(bf16), "w_attn_proj": w_attn_proj[l].astype(bf16),
            "w_out": w_out[l].astype(bf16),
            "ffn2_norm": ffn2_norm[l][None], "ffn2_wg": ffn2_w_gate[l].astype(bf16),
            "ffn2_wu": ffn2_w_up[l].astype(bf16), "ffn2_wd": ffn2_w_down[l].astype(bf16),
        }
        yp, *sp = _prompt_layer(yp, p)
        ys, *ss = _sample_layer(ys, state_conv[l], state_ssm[l], cache_k[l], cache_v[l], cache_logf[l],
                                page_table, p)
        prompt_states.append(sp)
        sample_states.append(ss)
    k_p, v_p, logf_p, ssm_p, conv_p = (jnp.stack(t) for t in zip(*prompt_states))
    k_s, v_s, logf_s, ssm_s, conv_s = (jnp.stack(t) for t in zip(*sample_states))
    return (yp, ys, k_p, v_p, logf_p, ssm_p, conv_p, k_s, v_s, logf_s, ssm_s, conv_s)
```

```python
import functools
import numpy as np
import jax
import jax.numpy as jnp
from jax import lax
from jax.experimental import pallas as pl
from jax.experimental.pallas import tpu as pltpu

f32 = jnp.float32
bf16 = jnp.bfloat16

RMS_EPS = 1e-6
D_MODEL = 1024
D_FF = 2816
D_SSM = 2048
SSD_HEAD_DIM = 64
SSD_HEADS = 32
SSD_GROUPS = 4
HEADS_PER_GROUP = SSD_HEADS // SSD_GROUPS
D_STATE = 128
D_CONV = 4
CONV_DIM = D_SSM + 2 * SSD_GROUPS * D_STATE
SSD_CHUNK = 128
ATT_HEAD_DIM = 64
ATT_HEADS = 16
KV_HEADS = 4
Q_PER_KV = ATT_HEADS // KV_HEADS
ATT_DIM = ATT_HEADS * ATT_HEAD_DIM
KV_DIM = KV_HEADS * ATT_HEAD_DIM
NEG_BIG = -1e30

COL_Z = 0
COL_GS = 2048
COL_GA = 3072
COL_Q = 4096
COL_KV = 5120
COL_XBC = 6144
W_MAIN_COLS = 9216

VMEM_LIMIT = 56 * 1024 * 1024


def _cparams(sem):
    return pltpu.CompilerParams(dimension_semantics=sem, vmem_limit_bytes=VMEM_LIMIT)


def _split2(x):
    hi = x.astype(bf16)
    lo = (x - hi.astype(f32)).astype(bf16)
    return hi, lo


def _split3(x):
    hi = x.astype(bf16)
    r = x - hi.astype(f32)
    mid = r.astype(bf16)
    lo = (r - mid.astype(f32)).astype(bf16)
    return hi, mid, lo


def _dot(a, b):
    return jnp.dot(a, b, preferred_element_type=f32)


def _dot_nt(a, b):
    return lax.dot_general(a, b, (((1,), (1,)), ((), ())), preferred_element_type=f32)


def _dot_tn(a, b):
    return lax.dot_general(a, b, (((0,), (0,)), ((), ())), preferred_element_type=f32)


def _dot2_lhs(x, m):
    hi, lo = _split2(x)
    return _dot(hi, m) + _dot(lo, m)


FFN_CHUNK = 256


def _rms(x, w):
    return x * lax.rsqrt(jnp.mean(x * x, axis=-1, keepdims=True) + RMS_EPS) * w


def _ffn_body(x_ref, nw_ref, wg_ref, wu_ref, wd_ref, nw2_ref, o_ref, *rest, emit_norm):
    a_ref = rest[-1]
    xn = _rms(x_ref[...], nw_ref[...]).astype(bf16)
    for c in range(D_FF // FFN_CHUNK):
        cols = slice(c * FFN_CHUNK, (c + 1) * FFN_CHUNK)
        g = _dot(xn, wg_ref[:, cols])
        u = _dot(xn, wu_ref[:, cols])
        a_ref[:, cols] = ((g * jax.nn.sigmoid(g)) * u).astype(bf16)
    y = x_ref[...] + 0.5 * _dot(a_ref[...], wd_ref[...])
    o_ref[...] = y
    if emit_norm:
        rest[0][...] = _rms(y, nw2_ref[...]).astype(bf16)


def _ffn(x, nw, wg, wu, wd, tm, nw2=None):
    n = x.shape[0]
    emit_norm = nw2 is not None
    full = lambda shp: pl.BlockSpec(shp, lambda i: (0, 0))
    row = pl.BlockSpec((tm, D_MODEL), lambda i: (i, 0))
    out_shape = [jax.ShapeDtypeStruct((n, D_MODEL), f32)]
    if emit_norm:
        out_shape.append(jax.ShapeDtypeStruct((n, D_MODEL), bf16))
    outs = pl.pallas_call(
        functools.partial(_ffn_body, emit_norm=emit_norm),
        grid=(n // tm,),
        in_specs=[row, full((1, D_MODEL)), full((D_MODEL, D_FF)), full((D_MODEL, D_FF)), full((D_FF, D_MODEL)),
                  full((1, D_MODEL))],
        out_specs=[row] * len(out_shape),
        out_shape=out_shape,
        scratch_shapes=[pltpu.VMEM((tm, D_FF), bf16)],
        compiler_params=_cparams(("parallel",)),
        name="ffn",
    )(x, nw, wg, wu, wd, nw2 if emit_norm else nw)
    return outs if emit_norm else outs[0]


def _proj_body(x_ref, w_ref, o_ref):
    o_ref[...] = _dot(x_ref[...], w_ref[...])


def _in_proj(xn, w, tm, tn):
    n = xn.shape[0]
    c = w.shape[1]
    return pl.pallas_call(
        _proj_body,
        grid=(c // tn, n // tm),
        in_specs=[
            pl.BlockSpec((tm, D_MODEL), lambda j, i: (i, 0)),
            pl.BlockSpec((D_MODEL, tn), lambda j, i: (0, j)),
        ],
        out_specs=pl.BlockSpec((tm, tn), lambda j, i: (i, j)),
        out_shape=jax.ShapeDtypeStruct((n, c), f32),
        compiler_params=_cparams(("parallel", "parallel")),
        name="in_proj",
    )(xn, w)


AUG_DIM = 128
BIAS_PIECES = 3


def _qkv_body(q_ref, kv_ref, qw_ref, kw_ref, bf_ref, seg_ref, segt_ref, tril_ref, pk_ref, pc_ref,
              qt_ref, ka_ref, vt_ref, ko_ref, vo_ref, lf_ref, dtp_ref, carry_ref):
    @pl.when(pl.program_id(1) == 0)
    def _():
        carry_ref[...] = jnp.zeros_like(carry_ref)

    seg = seg_ref[...]
    segt = segt_ref[...]
    tl = q_ref.shape[0]

    q = q_ref[...]
    msq = _dot2_lhs(q * q, seg)
    fac = _dot2_lhs(lax.rsqrt(msq + RMS_EPS), segt)
    qnt = (q * fac * qw_ref[...]).T
    row = lax.broadcasted_iota(jnp.int32, (AUG_DIM - ATT_HEAD_DIM, tl), 0)
    for h in range(ATT_HEADS):
        qt_ref[h, 0:ATT_HEAD_DIM, :] = qnt[h * ATT_HEAD_DIM:(h + 1) * ATT_HEAD_DIM, :].astype(bf16)
        lo = BIAS_PIECES * (h % Q_PER_KV)
        qt_ref[h, ATT_HEAD_DIM:AUG_DIM, :] = jnp.where((row >= lo) & (row < lo + BIAS_PIECES), 1.0, 0.0).astype(bf16)

    kv = kv_ref[...]
    k = kv[:, 0:KV_DIM]
    msk = _dot2_lhs(k * k, seg[0:KV_DIM, :])
    fack = _dot2_lhs(lax.rsqrt(msk + RMS_EPS), segt[:, 0:KV_DIM])
    kn = k * fack * kw_ref[...]
    ko_ref[...] = kn
    v = kv[:, KV_DIM:2 * KV_DIM]
    vo_ref[...] = v
    vt_ref[...] = v.T.astype(bf16)
    dtp_ref[...] = kv[:, 2 * KV_DIM:2 * KV_DIM + 128]
    f_raw = kv[:, 2 * KV_DIM + SSD_HEADS:2 * KV_DIM + SSD_HEADS + ATT_HEADS]
    lf = jax.nn.log_sigmoid(f_raw + bf_ref[...])
    lf_ref[...] = lf

    tril = tril_ref[...]
    p0, p1, p2 = _split3(lf)
    c = _dot(tril, p0) + _dot(tril, p1) + _dot(tril, p2) + carry_ref[...]
    carry_ref[...] = c[tl - 1:tl, :]
    n0, n1, n2 = _split3(-c)
    ka = _dot(kn.astype(bf16), pk_ref[...]) + _dot(n0, pc_ref[0]) + _dot(n1, pc_ref[1]) + _dot(n2, pc_ref[2])
    for g in range(KV_HEADS):
        ka_ref[g] = ka[:, g * AUG_DIM:(g + 1) * AUG_DIM].astype(bf16)


def _qkv_post(main3, qw, kw, b_f, tl):
    bsz, seqlen, _ = main3.shape
    seg = np.zeros((ATT_DIM, 128), np.float32)
    for h in range(ATT_HEADS):
        seg[h * ATT_HEAD_DIM:(h + 1) * ATT_HEAD_DIM, h] = 1.0 / ATT_HEAD_DIM
    segt = (seg.T > 0).astype(np.float32)
    tril = np.tril(np.ones((tl, tl), np.float32))
    pk = np.zeros((KV_DIM, KV_HEADS * AUG_DIM), np.float32)
    pc = np.zeros((BIAS_PIECES, ATT_HEADS, KV_HEADS * AUG_DIM), np.float32)
    for g in range(KV_HEADS):
        for d in range(ATT_HEAD_DIM):
            pk[g * ATT_HEAD_DIM + d, g * AUG_DIM + d] = 1.0
        for j in range(Q_PER_KV):
            for i in range(BIAS_PIECES):
                pc[i, g * Q_PER_KV + j, g * AUG_DIM + ATT_HEAD_DIM + BIAS_PIECES * j + i] = 1.0
    full = lambda shp: pl.BlockSpec(shp, lambda b, i: tuple(0 for _ in shp))
    outs = pl.pallas_call(
        _qkv_body,
        grid=(bsz, seqlen // tl),
        in_specs=[
            pl.BlockSpec((None, tl, 1024), lambda b, i: (b, i, COL_Q // 1024)),
            pl.BlockSpec((None, tl, 1024), lambda b, i: (b, i, COL_KV // 1024)),
            full((1, ATT_DIM)), full((1, KV_DIM)), full((1, ATT_HEADS)),
            full((ATT_DIM, 128)), full((128, ATT_DIM)), full((tl, tl)),
            full((KV_DIM, KV_HEADS * AUG_DIM)), full((BIAS_PIECES, ATT_HEADS, KV_HEADS * AUG_DIM)),
        ],
        out_specs=[
            pl.BlockSpec((None, ATT_HEADS, AUG_DIM, tl), lambda b, i: (b, 0, 0, i)),
            pl.BlockSpec((None, KV_HEADS, tl, AUG_DIM), lambda b, i: (b, 0, i, 0)),
            pl.BlockSpec((None, KV_DIM, tl), lambda b, i: (b, 0, i)),
            pl.BlockSpec((None, tl, KV_DIM), lambda b, i: (b, i, 0)),
            pl.BlockSpec((None, tl, KV_DIM), lambda b, i: (b, i, 0)),
            pl.BlockSpec((None, tl, ATT_HEADS), lambda b, i: (b, i, 0)),
            pl.BlockSpec((None, tl, 128), lambda b, i: (b, i, 0)),
        ],
        out_shape=[
            jax.ShapeDtypeStruct((bsz, ATT_HEADS, AUG_DIM, seqlen), bf16),
            jax.ShapeDtypeStruct((bsz, KV_HEADS, seqlen, AUG_DIM), bf16),
            jax.ShapeDtypeStruct((bsz, KV_DIM, seqlen), bf16),
            jax.ShapeDtypeStruct((bsz, seqlen, KV_DIM), f32),
            jax.ShapeDtypeStruct((bsz, seqlen, KV_DIM), f32),
            jax.ShapeDtypeStruct((bsz, seqlen, ATT_HEADS), f32),
            jax.ShapeDtypeStruct((bsz, seqlen, 128), f32),
        ],
        scratch_shapes=[pltpu.VMEM((1, ATT_HEADS), f32)],
        compiler_params=_cparams(("parallel", "arbitrary")),
        name="qkv_post",
    )(main3, main3, qw, kw, b_f, jnp.asarray(seg, bf16), jnp.asarray(segt, bf16), jnp.asarray(tril, bf16),
      jnp.asarray(pk, bf16), jnp.asarray(pc, bf16))
    return outs


def _ssd_body(xbc_ref, z_ref, dtp_ref, prev_ref, h0_ref, cw_ref, cb_ref, dtb_ref, alog_ref, dskip_ref,
              nrm_ref, e2_ref, tril_ref, triu_ref, y_ref, hnew_ref, cnew_ref, xpad_ref, hs_ref, *, qr, q):
    c = pl.program_id(1)
    last = pl.num_programs(1) - 1

    @pl.when(c == 0)
    def _():
        xpad_ref[0:8, :] = jnp.zeros((8, CONV_DIM), f32)
        xpad_ref[8 - (D_CONV - 1):8, :] = prev_ref[...]
        hs_ref[...] = h0_ref[...]

    x = xbc_ref[...]
    xpad_ref[8:8 + qr, :] = x
    xa = xpad_ref[...]
    conv = cb_ref[...] + cw_ref[D_CONV - 1:D_CONV, :] * x
    for j in range(D_CONV - 1):
        shifted = pltpu.roll(xa, D_CONV - 1 - j, axis=0)[8:8 + qr, :]
        conv = conv + cw_ref[j:j + 1, :] * shifted
    xc = conv * jax.nn.sigmoid(conv)

    @pl.when(c == last)
    def _():
        cnew_ref[...] = xpad_ref[qr + 5:qr + 8, :]

    xpad_ref[0:8, :] = xpad_ref[qr:qr + 8, :]

    dt = jax.nn.softplus(dtp_ref[...][:, 0:SSD_HEADS] + dtb_ref[...])
    zz = z_ref[...]
    if q != qr:
        xc = jnp.concatenate([xc, jnp.zeros((q - qr, CONV_DIM), f32)], axis=0)
        dt = jnp.concatenate([dt, jnp.zeros((q - qr, SSD_HEADS), f32)], axis=0)
        zz = jnp.concatenate([zz, jnp.zeros((q - qr, D_SSM), f32)], axis=0)
    xs = xc[:, 0:D_SSM]
    bm = xc[:, D_SSM:D_SSM + SSD_GROUPS * D_STATE].astype(bf16)
    cm = xc[:, D_SSM + SSD_GROUPS * D_STATE:CONV_DIM].astype(bf16)

    a = dt * (-jnp.exp(alog_ref[...]))
    tril = tril_ref[...]
    a0, a1, a2 = _split3(a)
    acs = _dot(tril, a0) + _dot(tril, a1) + _dot(tril, a2)
    triu = triu_ref[...]
    acst = _dot_tn(a0, triu) + _dot_tn(a1, triu) + _dot_tn(a2, triu)
    acs_last = acs[q - 1:q, :]
    fac = jnp.concatenate([dt, jnp.exp(acs_last - acs), jnp.exp(acs)], axis=0)
    fac_e = _dot(jnp.concatenate(_split2(fac), axis=1), e2_ref[...])
    xdt = xs * fac_e[0:q]
    xdtw = (xdt * fac_e[q:2 * q]).astype(bf16)
    eacs_e = fac_e[2 * q:3 * q]
    edec = jnp.exp(acst[:, q - 1:q])

    li = lax.broadcasted_iota(jnp.int32, (q, q), 0)
    si = lax.broadcasted_iota(jnp.int32, (q, q), 1)
    causal = li >= si
    lane = lax.broadcasted_iota(jnp.int32, (q, 2 * SSD_HEAD_DIM), 1)
    left = lane < SSD_HEAD_DIM

    y_groups = []
    for g in range(SSD_GROUPS):
        bg = bm[:, g * D_STATE:(g + 1) * D_STATE]
        cg = cm[:, g * D_STATE:(g + 1) * D_STATE]
        cbm = _dot_nt(cg, bg)
        r0 = g * HEADS_PER_GROUP * SSD_HEAD_DIM
        hg = hs_ref[r0:r0 + HEADS_PER_GROUP * SSD_HEAD_DIM, :].astype(bf16)
        y_off = _dot_nt(cg, hg)
        st = _dot_tn(xdtw[:, r0:r0 + HEADS_PER_GROUP * SSD_HEAD_DIM], bg)
        for hh in range(HEADS_PER_GROUP):
            h = g * HEADS_PER_GROUP + hh
            rows = slice(h * SSD_HEAD_DIM, (h + 1) * SSD_HEAD_DIM)
            hs_ref[rows, :] = (hs_ref[rows, :] * edec[h:h + 1, :]
                               + st[hh * SSD_HEAD_DIM:(hh + 1) * SSD_HEAD_DIM, :])
        pairs = []
        for pr in range(HEADS_PER_GROUP // 2):
            h0i = g * HEADS_PER_GROUP + 2 * pr
            xpair = xdt[:, h0i * SSD_HEAD_DIM:(h0i + 2) * SSD_HEAD_DIM]
            acc = None
            for t in range(2):
                h = h0i + t
                sg = acs[:, h:h + 1] - acst[h:h + 1, :]
                m = (cbm * jnp.exp(jnp.where(causal, sg, NEG_BIG))).astype(bf16)
                xm = jnp.where(left if t == 0 else jnp.logical_not(left), xpair, 0.0).astype(bf16)
                d = _dot(m, xm)
                acc = d if acc is None else acc + d
            pairs.append(acc)
        y_diag = jnp.concatenate(pairs, axis=1)
        y_groups.append(y_diag + y_off * eacs_e[:, r0:r0 + HEADS_PER_GROUP * SSD_HEAD_DIM])
    y = jnp.concatenate(y_groups, axis=1) + dskip_ref[...] * xs

    @pl.when(c == last)
    def _():
        hnew_ref[...] = hs_ref[...]

    yg = y * (zz * jax.nn.sigmoid(zz))
    gw = D_SSM // SSD_GROUPS
    outs = []
    for g in range(SSD_GROUPS):
        t = yg[:, g * gw:(g + 1) * gw]
        outs.append(t * lax.rsqrt(jnp.mean(t * t, axis=-1, keepdims=True) + RMS_EPS))
    yn = jnp.concatenate(outs, axis=1) * nrm_ref[...]
    y_ref[...] = yn[0:qr, :].astype(bf16)


def _ssd(main3, dtp, conv_prev, h0, conv_w, conv_b, dt_bias, a_log, d_skip_e, ssd_norm, qr, q):
    bsz, seqlen, _ = main3.shape
    nc = seqlen // qr
    e = np.zeros((SSD_HEADS, D_SSM), np.float32)
    for h in range(SSD_HEADS):
        e[h, h * SSD_HEAD_DIM:(h + 1) * SSD_HEAD_DIM] = 1.0
    tril = np.tril(np.ones((q, q), np.float32))
    nst = SSD_HEADS * SSD_HEAD_DIM
    full = lambda shp: pl.BlockSpec(shp, lambda b, c: tuple(0 for _ in shp))
    y, hnew, cnew = pl.pallas_call(
        functools.partial(_ssd_body, qr=qr, q=q),
        grid=(bsz, nc),
        in_specs=[
            pl.BlockSpec((None, qr, CONV_DIM), lambda b, c: (b, c, COL_XBC // CONV_DIM)),
            pl.BlockSpec((None, qr, D_SSM), lambda b, c: (b, c, COL_Z // D_SSM)),
            pl.BlockSpec((None, qr, 128), lambda b, c: (b, c, 0)),
            pl.BlockSpec((None, D_CONV - 1, CONV_DIM), lambda b, c: (b, 0, 0)),
            pl.BlockSpec((None, nst, D_STATE), lambda b, c: (b, 0, 0)),
            full((D_CONV, CONV_DIM)), full((1, CONV_DIM)), full((1, SSD_HEADS)), full((1, SSD_HEADS)),
            full((1, D_SSM)), full((1, D_SSM)), full((2 * SSD_HEADS, D_SSM)),
            full((q, q)), full((q, q)),
        ],
        out_specs=[
            pl.BlockSpec((None, qr, D_SSM), lambda b, c: (b, c, 0)),
            pl.BlockSpec((None, nst, D_STATE), lambda b, c: (b, 0, 0)),
            pl.BlockSpec((None, D_CONV - 1, CONV_DIM), lambda b, c: (b, 0, 0)),
        ],
        out_shape=[
            jax.ShapeDtypeStruct((bsz, seqlen, D_SSM), bf16),
            jax.ShapeDtypeStruct((bsz, nst, D_STATE), f32),
            jax.ShapeDtypeStruct((bsz, D_CONV - 1, CONV_DIM), f32),
        ],
        scratch_shapes=[pltpu.VMEM((qr + 8, CONV_DIM), f32), pltpu.VMEM((nst, D_STATE), f32)],
        compiler_params=_cparams(("parallel", "arbitrary")),
        name="ssd",
    )(main3, main3, dtp, conv_prev, h0, conv_w, conv_b, dt_bias, a_log, d_skip_e, ssd_norm,
      jnp.asarray(np.concatenate([e, e], axis=0), bf16), jnp.asarray(tril, bf16), jnp.asarray(tril.T, bf16))
    return y, hnew, cnew


FLASH_Q_LANES = 256
FLASH_KEY_ROWS = 128
FLASH_LOOKAHEAD = 4


def _flash_body(qi_ref, ki_ref, qt_ref, ka_ref, vt_ref, o_ref, m_ref, l_ref, acc_ref, *, tb):
    t = pl.program_id(2)
    qi = qi_ref[t]
    ki = ki_ref[t]
    ql, kr = FLASH_Q_LANES, FLASH_KEY_ROWS

    @pl.when(ki == 0)
    def _():
        m_ref[...] = jnp.full_like(m_ref, NEG_BIG)
        l_ref[...] = jnp.zeros_like(l_ref)
        acc_ref[...] = jnp.zeros_like(acc_ref)

    def process(diagonal):
        key_i = lax.broadcasted_iota(jnp.int32, (kr, ql), 0)
        qry_i = lax.broadcasted_iota(jnp.int32, (kr, ql), 1)
        tiles = [(j, r, kb) for kb in range(tb // kr) for j in range(Q_PER_KV) for r in range(tb // ql)
                 if not (diagonal and kb * kr > r * ql + ql - 1)]

        def scores(tile):
            j, r, kb = tile
            return _dot(ka_ref[kb * kr:(kb + 1) * kr, :], qt_ref[j, :, r * ql:(r + 1) * ql])

        def finish(tile, st):
            j, r, kb = tile
            cols = slice(r * ql, (r + 1) * ql)
            if diagonal and kb * kr + kr - 1 > r * ql:
                st = jnp.where(key_i + kb * kr <= qry_i + r * ql, st, NEG_BIG)
            m_prev = m_ref[j, :, cols]
            m_new = jnp.maximum(m_prev, jnp.max(st, axis=0, keepdims=True))
            alpha = jnp.exp(m_prev - m_new)
            p = jnp.exp(st - m_new)
            l_ref[j, :, cols] = alpha * l_ref[j, :, cols] + jnp.sum(p, axis=0, keepdims=True)
            acc_ref[j, :, cols] = (alpha * acc_ref[j, :, cols]
                                   + _dot(vt_ref[:, kb * kr:(kb + 1) * kr], p.astype(bf16)))
            m_ref[j, :, cols] = m_new

        pending = [scores(tl_) for tl_ in tiles[:FLASH_LOOKAHEAD]]
        for i, tile in enumerate(tiles):
            if i + FLASH_LOOKAHEAD < len(tiles):
                pending.append(scores(tiles[i + FLASH_LOOKAHEAD]))
            finish(tile, pending.pop(0))

    pl.when(ki < qi)(lambda: process(False))

    @pl.when(ki == qi)
    def _():
        process(True)
        o = acc_ref[...] / l_ref[...]
        o_ref[...] = jnp.concatenate([o[j].T for j in range(Q_PER_KV)], axis=1).astype(bf16)


def _flash_fox(qt, ka, vt, tb):
    bsz, _, _, seqlen = qt.shape
    nb = seqlen // tb
    qi_tab = np.array([i for i in range(nb) for j in range(i + 1)], np.int32)
    ki_tab = np.array([j for i in range(nb) for j in range(i + 1)], np.int32)
    grid_spec = pltpu.PrefetchScalarGridSpec(
        num_scalar_prefetch=2,
        grid=(bsz, KV_HEADS, qi_tab.shape[0]),
        in_specs=[
            pl.BlockSpec((None, Q_PER_KV, AUG_DIM, tb), lambda b, g, t, qi, ki: (b, g, 0, qi[t])),
            pl.BlockSpec((None, None, tb, AUG_DIM), lambda b, g, t, qi, ki: (b, g, ki[t], 0)),
            pl.BlockSpec((None, None, ATT_HEAD_DIM, tb), lambda b, g, t, qi, ki: (b, g, 0, ki[t])),
        ],
        out_specs=pl.BlockSpec((None, tb, Q_PER_KV * ATT_HEAD_DIM), lambda b, g, t, qi, ki: (b, qi[t], g)),
        scratch_shapes=[
            pltpu.VMEM((Q_PER_KV, 1, tb), f32),
            pltpu.VMEM((Q_PER_KV, 1, tb), f32),
            pltpu.VMEM((Q_PER_KV, ATT_HEAD_DIM, tb), f32),
        ],
    )
    return pl.pallas_call(
        functools.partial(_flash_body, tb=tb),
        grid_spec=grid_spec,
        out_shape=jax.ShapeDtypeStruct((bsz, seqlen, ATT_DIM), bf16),
        compiler_params=_cparams(("parallel", "parallel", "arbitrary")),
        name="flash_fox",
    )(jnp.asarray(qi_tab), jnp.asarray(ki_tab), qt, ka, vt)


PAGES_PER_STEP = 16


def _lane_cumsum(x):
    n = x.shape[1]
    lane = lax.broadcasted_iota(jnp.int32, x.shape, 1)
    sh = 1
    while sh < n:
        x = x + jnp.where(lane >= sh, pltpu.roll(x, sh, axis=1), 0.0)
        sh *= 2
    return x


PAGE_SLOTS = 3


def _paged_body(pt_ref, qbd_ref, kn_ref, vn_ref, lfn_ref, ck_hbm, cv_hbm, clf_hbm, o_ref,
                kbuf, vbuf, lfbuf, sem, m_ref, l_ref, acc_ref, carry_ref, *, dec_seq):
    npg = PAGES_PER_STEP
    b = pl.program_id(0)
    j = pl.program_id(1)
    nj = pl.num_programs(1)
    step = b * nj + j
    n_steps = pl.num_programs(0) * nj
    rows = qbd_ref.shape[0]

    def page_copies(s):
        sb = s // nj
        sj = s - sb * nj
        slot = s % PAGE_SLOTS
        for i in range(npg):
            pid = pt_ref[sb, sj * npg + i]
            yield pltpu.make_async_copy(ck_hbm.at[pid], kbuf.at[slot, i], sem.at[0, slot])
            yield pltpu.make_async_copy(cv_hbm.at[pid], vbuf.at[slot, i], sem.at[1, slot])
            yield pltpu.make_async_copy(clf_hbm.at[pid], lfbuf.at[slot, i], sem.at[2, slot])

    @pl.when(step == 0)
    def _():
        for s in range(PAGE_SLOTS - 1):
            for cp in page_copies(s):
                cp.start()

    @pl.when(j == 0)
    def _():
        m_ref[...] = jnp.full_like(m_ref, NEG_BIG)
        l_ref[...] = jnp.zeros_like(l_ref)
        acc_ref[...] = jnp.zeros_like(acc_ref)
        carry_ref[...] = jnp.zeros_like(carry_ref)

    for cp in page_copies(step):
        cp.wait()
    slot = step % PAGE_SLOTS

    qbd = qbd_ref[...]

    def attend(kt, vt, lft, mask):
        nkeys = kt.shape[1]
        ct = _lane_cumsum(lft) + carry_ref[...]
        carry_ref[...] = ct[:, nkeys - 1:nkeys]
        s = _dot(qbd, kt.astype(bf16))
        s3 = s.reshape(ATT_HEADS, rows // ATT_HEADS, nkeys) - ct[:, None, :]
        s = s3.reshape(rows, nkeys)
        if mask is not None:
            s = jnp.where(mask, s, NEG_BIG)
        m_prev = m_ref[...]
        m_new = jnp.maximum(m_prev, jnp.max(s, axis=-1, keepdims=True))
        alpha = jnp.exp(m_prev - m_new)
        p = jnp.exp(s - m_new)
        l_ref[...] = alpha * l_ref[...] + jnp.sum(p, axis=-1, keepdims=True)
        acc_ref[...] = alpha * acc_ref[...] + _dot_nt(p.astype(bf16), vt.astype(bf16))
        m_ref[...] = m_new

    kt = jnp.concatenate([kbuf[slot, i].reshape(KV_DIM, -1) for i in range(npg)], axis=1)
    vt = jnp.concatenate([vbuf[slot, i].reshape(KV_DIM, -1) for i in range(npg)], axis=1)
    lft = jnp.concatenate([lfbuf[slot, i] for i in range(npg)], axis=1)
    attend(kt, vt, lft, None)

    @pl.when(j == nj - 1)
    def _():
        nk = kn_ref.shape[1]
        r = lax.broadcasted_iota(jnp.int32, (rows, nk), 0) % dec_seq
        t = lax.broadcasted_iota(jnp.int32, (rows, nk), 1)
        attend(kn_ref[...], vn_ref[...], lfn_ref[...], t <= r)
        o_ref[...] = acc_ref[...] / l_ref[...]

    @pl.when(step + (PAGE_SLOTS - 1) < n_steps)
    def _():
        for cp in page_copies(step + (PAGE_SLOTS - 1)):
            cp.start()


def _paged_fox(page_table, qbd, knt, vnt, lfnt, ckt, cvt, clft, dec_seq):
    bsz, n_pages = page_table.shape
    rows = qbd.shape[1]
    page = ckt.shape[3]
    npg = PAGES_PER_STEP
    assert n_pages % npg == 0 and bsz * (n_pages // npg) >= PAGE_SLOTS - 1
    bmap = lambda b, j, pt: (b, 0, 0)
    hbm = pl.BlockSpec(memory_space=pl.ANY)
    grid_spec = pltpu.PrefetchScalarGridSpec(
        num_scalar_prefetch=1,
        grid=(bsz, n_pages // npg),
        in_specs=[
            pl.BlockSpec((None, rows, KV_DIM), bmap),
            pl.BlockSpec((None, KV_DIM, knt.shape[2]), bmap),
            pl.BlockSpec((None, KV_DIM, vnt.shape[2]), bmap),
            pl.BlockSpec((None, ATT_HEADS, lfnt.shape[2]), bmap),
            hbm, hbm, hbm,
        ],
        out_specs=pl.BlockSpec((None, rows, KV_DIM), bmap),
        scratch_shapes=[
            pltpu.VMEM((PAGE_SLOTS, npg, KV_HEADS, ATT_HEAD_DIM, page), f32),
            pltpu.VMEM((PAGE_SLOTS, npg, KV_HEADS, ATT_HEAD_DIM, page), f32),
            pltpu.VMEM((PAGE_SLOTS, npg, ATT_HEADS, page), f32),
            pltpu.SemaphoreType.DMA((3, PAGE_SLOTS)),
            pltpu.VMEM((rows, 1), f32), pltpu.VMEM((rows, 1), f32), pltpu.VMEM((rows, KV_DIM), f32),
            pltpu.VMEM((ATT_HEADS, 1), f32),
        ],
    )
    return pl.pallas_call(
        functools.partial(_paged_body, dec_seq=dec_seq),
        grid_spec=grid_spec,
        out_shape=jax.ShapeDtypeStruct((bsz, rows, KV_DIM), f32),
        compiler_params=_cparams(("arbitrary", "arbitrary")),
        name="paged_fox",
    )(page_table, qbd, knt, vnt, lfnt, ckt, cvt, clft)


def _merge_body(ys_ref, ya_ref, gs_ref, ga_ref, h_ref, ws_ref, wa_ref, wo_ref, o_ref):
    a = jax.nn.sigmoid(gs_ref[...]) * _dot(ys_ref[...], ws_ref[...])
    b = jax.nn.sigmoid(ga_ref[...]) * _dot(ya_ref[...], wa_ref[...])
    o_ref[...] = h_ref[...] + _dot((a + b).astype(bf16), wo_ref[...])


def _merge(y_ssd, y_att, main, h, ws, wa, wo, tm):
    n = h.shape[0]
    full = lambda shp: pl.BlockSpec(shp, lambda i: (0, 0))
    return pl.pallas_call(
        _merge_body,
        grid=(n // tm,),
        in_specs=[
            pl.BlockSpec((tm, D_SSM), lambda i: (i, 0)),
            pl.BlockSpec((tm, ATT_DIM), lambda i: (i, 0)),
            pl.BlockSpec((tm, D_MODEL), lambda i: (i, COL_GS // D_MODEL)),
            pl.BlockSpec((tm, D_MODEL), lambda i: (i, COL_GA // D_MODEL)),
            pl.BlockSpec((tm, D_MODEL), lambda i: (i, 0)),
            full((D_SSM, D_MODEL)), full((ATT_DIM, D_MODEL)), full((D_MODEL, D_MODEL)),
        ],
        out_specs=pl.BlockSpec((tm, D_MODEL), lambda i: (i, 0)),
        out_shape=jax.ShapeDtypeStruct((n, D_MODEL), f32),
        compiler_params=_cparams(("parallel",)),
        name="merge",
    )(y_ssd, y_att, main, main, h, ws, wa, wo)


def _pack_w_in(w_in):
    o = np.cumsum([0, D_SSM, CONV_DIM, SSD_HEADS, ATT_DIM, KV_DIM, KV_DIM, ATT_HEADS, D_MODEL, D_MODEL])
    z, xbc, dt, q, k, v, f, gs, ga = (w_in[:, int(o[i]):int(o[i + 1])] for i in range(9))
    pad = jnp.zeros((D_MODEL, COL_XBC - COL_KV - 2 * KV_DIM - SSD_HEADS - ATT_HEADS), w_in.dtype)
    w = jnp.concatenate([z, gs, ga, q, k, v, dt, f, pad, xbc], axis=1)
    return w.astype(bf16)


def _dense_pre(x2, p, tm):
    h, hn = _ffn(x2, p["ffn1_norm"], p["ffn1_wg"], p["ffn1_wu"], p["ffn1_wd"], tm, p["mix_norm"])
    main = _in_proj(hn, p["w_main"], min(x2.shape[0], 1024), W_MAIN_COLS // 4)
    return h, main


def _dense_post(y_ssd, y_att, main, h, p, tm):
    h2 = _merge(y_ssd, y_att, main, h, p["w_ssd_proj"], p["w_attn_proj"], p["w_out"], tm)
    return _ffn(h2, p["ffn2_norm"], p["ffn2_wg"], p["ffn2_wu"], p["ffn2_wd"], tm)


def _ssd_args(p):
    return (p["conv_w"], p["conv_b"], p["dt_bias"], p["a_log"], p["d_skip_e"], p["ssd_norm"])


def _prompt_layer(x, p):
    bsz, seqlen, _ = x.shape
    n = bsz * seqlen
    tm = 512
    h, main = _dense_pre(x.reshape(n, D_MODEL), p, tm)
    main3 = main.reshape(bsz, seqlen, W_MAIN_COLS)
    qt, ka, vt, k, v, logf, dtp = _qkv_post(main3, p["q_norm_e"], p["k_norm_e"], p["b_f"], 512)
    conv0 = jnp.zeros((bsz, D_CONV - 1, CONV_DIM), f32)
    h0 = jnp.zeros((bsz, SSD_HEADS * SSD_HEAD_DIM, D_STATE), f32)
    y_ssd, h_new, conv_new = _ssd(main3, dtp, conv0, h0, *_ssd_args(p), SSD_CHUNK, SSD_CHUNK)
    y_att = _flash_fox(qt, ka, vt.reshape(bsz, KV_HEADS, ATT_HEAD_DIM, seqlen), min(seqlen, 1024))
    y = _dense_post(y_ssd.reshape(n, D_SSM), y_att.reshape(n, ATT_DIM), main, h, p, tm)
    return (y.reshape(bsz, seqlen, D_MODEL),
            k.reshape(bsz, seqlen, KV_HEADS, ATT_HEAD_DIM), v.reshape(bsz, seqlen, KV_HEADS, ATT_HEAD_DIM),
            logf, h_new.reshape(bsz, SSD_HEADS, SSD_HEAD_DIM, D_STATE), conv_new)


def _sample_layer(x, conv_prev, h0, ck, cv, clf, page_table, p):
    bsz, t, _ = x.shape
    n = bsz * t
    tm = n
    h, main = _dense_pre(x.reshape(n, D_MODEL), p, tm)
    qt, _, _, k, v, logf, dtp = _qkv_post(main.reshape(1, n, W_MAIN_COLS), p["q_norm_e"], p["k_norm_e"], p["b_f"], n)
    main3 = main.reshape(bsz, t, W_MAIN_COLS)
    y_ssd, h_new, conv_new = _ssd(main3, dtp.reshape(bsz, t, 128), conv_prev,
                                  h0.reshape(bsz, SSD_HEADS * SSD_HEAD_DIM, D_STATE), *_ssd_args(p), t, 16)
    qg = qt[0, :, 0:ATT_HEAD_DIM, :].reshape(KV_HEADS, Q_PER_KV, ATT_HEAD_DIM, bsz, t).transpose(3, 0, 1, 4, 2)
    eye = jnp.eye(KV_HEADS, dtype=bf16)
    qbd = (qg[:, :, :, :, None, :] * eye[None, :, None, None, :, None]).reshape(bsz, ATT_HEADS * t, KV_DIM)
    page = ck.shape[1]
    padt = lambda a: jnp.pad(a.reshape(bsz, t, -1).transpose(0, 2, 1), ((0, 0), (0, 0), (0, page - t)))
    o = _paged_fox(page_table, qbd, padt(k), padt(v), padt(logf),
                   ck.transpose(0, 2, 3, 1), cv.transpose(0, 2, 3, 1), clf.transpose(0, 2, 1), t)
    o6 = o.reshape(bsz, KV_HEADS, Q_PER_KV, t, KV_HEADS, ATT_HEAD_DIM)
    y_att = jnp.stack([o6[:, g, :, :, g, :] for g in range(KV_HEADS)], axis=1)
    y_att = y_att.transpose(0, 3, 1, 2, 4).reshape(n, ATT_DIM).astype(bf16)
    y = _dense_post(y_ssd.reshape(n, D_SSM), y_att, main, h, p, tm)
    return (y.reshape(bsz, t, D_MODEL),
            k.reshape(bsz, t, KV_HEADS, ATT_HEAD_DIM), v.reshape(bsz, t, KV_HEADS, ATT_HEAD_DIM),
            logf.reshape(bsz, t, ATT_HEADS), h_new.reshape(bsz, SSD_HEADS, SSD_HEAD_DIM, D_STATE), conv_new)


def kernel(x_prompt, x_sample, cache_k, cache_v, cache_logf, state_ssm, state_conv, page_table, ffn1_norm, ffn1_w_gate, ffn1_w_up, ffn1_w_down, mix_norm, w_in, conv_w, conv_b, dt_bias, a_log, d_skip, ssd_norm, q_norm, k_norm, b_f, w_ssd_proj, w_attn_proj, w_out, ffn2_norm, ffn2_w_gate, ffn2_w_up, ffn2_w_down):
    depth = w_in.shape[0]
    yp, ys = x_prompt, x_sample
    prompt_states, sample_states = [], []
    for l in range(depth):
        p = {
            "ffn1_norm": ffn1_norm[l][None], "ffn1_wg": ffn1_w_gate[l].astype(bf16),
            "ffn1_wu": ffn1_w_up[l].astype(bf16), "ffn1_wd": ffn1_w_down[l].astype(bf16),
            "mix_norm": mix_norm[l][None], "w_main": _pack_w_in(w_in[l]),
            "conv_w": conv_w[l], "conv_b": conv_b[l][None], "dt_bias": dt_bias[l][None], "a_log": a_log[l][None],
            "d_skip_e": jnp.repeat(d_skip[l], SSD_HEAD_DIM)[None], "ssd_norm": ssd_norm[l][None],
            "q_norm_e": (jnp.tile(q_norm[l], ATT_HEADS) * (ATT_HEAD_DIM ** -0.5))[None], "k_norm_e": jnp.tile(k_norm[l], KV_HEADS)[None],
            "b_f": b_f[l][None],
            "w_ssd_proj": w_ssd_proj[l].astype(bf16), "w_attn_proj": w_attn_proj[l].astype(bf16),
            "w_out": w_out[l].astype(bf16),
            "ffn2_norm": ffn2_norm[l][None], "ffn2_wg": ffn2_w_gate[l].astype(bf16),
            "ffn2_wu": ffn2_w_up[l].astype(bf16), "ffn2_wd": ffn2_w_down[l].astype(bf16),
        }
        yp, *sp = _prompt_layer(yp, p)
        ys, *ss = _sample_layer(ys, state_conv[l], state_ssm[l], cache_k[l], cache_v[l], cache_logf[l],
                                page_table, p)
        prompt_states.append(sp)
        sample_states.append(ss)
    k_p, v_p, logf_p, ssm_p, conv_p = (jnp.stack(t) for t in zip(*prompt_states))
    k_s, v_s, logf_s, ssm_s, conv_s = (jnp.stack(t) for t in zip(*sample_states))
    return (yp, ys, k_p, v_p, logf_p, ssm_p, conv_p, k_s, v_s, logf_s, ssm_s, conv_s)
```

```python
import functools
import numpy as np
import jax
import jax.numpy as jnp
from jax import lax
from jax.experimental import pallas as pl
from jax.experimental.pallas import tpu as pltpu

f32 = jnp.float32
bf16 = jnp.bfloat16

RMS_EPS = 1e-6
D_MODEL = 1024
D_FF = 2816
D_SSM = 2048
SSD_HEAD_DIM = 64
SSD_HEADS = 32
SSD_GROUPS = 4
HEADS_PER_GROUP = SSD_HEADS // SSD_GROUPS
D_STATE = 128
D_CONV = 4
CONV_DIM = D_SSM + 2 * SSD_GROUPS * D_STATE
SSD_CHUNK = 128
ATT_HEAD_DIM = 64
ATT_HEADS = 16
KV_HEADS = 4
Q_PER_KV = ATT_HEADS // KV_HEADS
ATT_DIM = ATT_HEADS * ATT_HEAD_DIM
KV_DIM = KV_HEADS * ATT_HEAD_DIM
NEG_BIG = -1e30

COL_Z = 0
COL_GS = 2048
COL_GA = 3072
COL_Q = 4096
COL_KV = 5120
COL_XBC = 6144
W_MAIN_COLS = 9216

VMEM_LIMIT = 56 * 1024 * 1024


def _cparams(sem):
    return pltpu.CompilerParams(dimension_semantics=sem, vmem_limit_bytes=VMEM_LIMIT)


def _split2(x):
    hi = x.astype(bf16)
    lo = (x - hi.astype(f32)).astype(bf16)
    return hi, lo


def _split3(x):
    hi = x.astype(bf16)
    r = x - hi.astype(f32)
    mid = r.astype(bf16)
    lo = (r - mid.astype(f32)).astype(bf16)
    return hi, mid, lo


def _dot(a, b):
    return jnp.dot(a, b, preferred_element_type=f32)


def _dot_nt(a, b):
    return lax.dot_general(a, b, (((1,), (1,)), ((), ())), preferred_element_type=f32)


def _dot_tn(a, b):
    return lax.dot_general(a, b, (((0,), (0,)), ((), ())), preferred_element_type=f32)


def _dot2_lhs(x, m):
    hi, lo = _split2(x)
    return _dot(hi, m) + _dot(lo, m)


FFN_CHUNK = 256


def _rms(x, w):
    return x * lax.rsqrt(jnp.mean(x * x, axis=-1, keepdims=True) + RMS_EPS) * w


def _ffn_body(x_ref, nw_ref, wg_ref, wu_ref, wd_ref, nw2_ref, o_ref, *rest, emit_norm):
    a_ref = rest[-1]
    xn = _rms(x_ref[...], nw_ref[...]).astype(bf16)
    for c in range(D_FF // FFN_CHUNK):
        cols = slice(c * FFN_CHUNK, (c + 1) * FFN_CHUNK)
        g = _dot(xn, wg_ref[:, cols])
        u = _dot(xn, wu_ref[:, cols])
        a_ref[:, cols] = ((g * jax.nn.sigmoid(g)) * u).astype(bf16)
    y = x_ref[...] + 0.5 * _dot(a_ref[...], wd_ref[...])
    o_ref[...] = y
    if emit_norm:
        rest[0][...] = _rms(y, nw2_ref[...]).astype(bf16)


def _ffn(x, nw, wg, wu, wd, tm, nw2=None):
    n = x.shape[0]
    emit_norm = nw2 is not None
    full = lambda shp: pl.BlockSpec(shp, lambda i: (0, 0))
    row = pl.BlockSpec((tm, D_MODEL), lambda i: (i, 0))
    out_shape = [jax.ShapeDtypeStruct((n, D_MODEL), f32)]
    if emit_norm:
        out_shape.append(jax.ShapeDtypeStruct((n, D_MODEL), bf16))
    outs = pl.pallas_call(
        functools.partial(_ffn_body, emit_norm=emit_norm),
        grid=(n // tm,),
        in_specs=[row, full((1, D_MODEL)), full((D_MODEL, D_FF)), full((D_MODEL, D_FF)), full((D_FF, D_MODEL)),
                  full((1, D_MODEL))],
        out_specs=[row] * len(out_shape),
        out_shape=out_shape,
        scratch_shapes=[pltpu.VMEM((tm, D_FF), bf16)],
        compiler_params=_cparams(("parallel",)),
        name="ffn",
    )(x, nw, wg, wu, wd, nw2 if emit_norm else nw)
    return outs if emit_norm else outs[0]


def _proj_body(x_ref, w_ref, o_ref):
    o_ref[...] = _dot(x_ref[...], w_ref[...])


def _in_proj(xn, w, tm, tn):
    n = xn.shape[0]
    c = w.shape[1]
    return pl.pallas_call(
        _proj_body,
        grid=(c // tn, n // tm),
        in_specs=[
            pl.BlockSpec((tm, D_MODEL), lambda j, i: (i, 0)),
            pl.BlockSpec((D_MODEL, tn), lambda j, i: (0, j)),
        ],
        out_specs=pl.BlockSpec((tm, tn), lambda j, i: (i, j)),
        out_shape=jax.ShapeDtypeStruct((n, c), f32),
        compiler_params=_cparams(("parallel", "parallel")),
        name="in_proj",
    )(xn, w)


AUG_DIM = 128
BIAS_PIECES = 3


def _qkv_body(q_ref, kv_ref, qw_ref, kw_ref, bf_ref, seg_ref, segt_ref, tril_ref, pk_ref, pc_ref,
              qt_ref, ka_ref, vt_ref, ko_ref, vo_ref, lf_ref, dtp_ref, carry_ref):
    @pl.when(pl.program_id(1) == 0)
    def _():
        carry_ref[...] = jnp.zeros_like(carry_ref)

    seg = seg_ref[...]
    segt = segt_ref[...]
    tl = q_ref.shape[0]

    q = q_ref[...]
    msq = _dot2_lhs(q * q, seg)
    fac = _dot2_lhs(lax.rsqrt(msq + RMS_EPS), segt)
    qnt = (q * fac * qw_ref[...]).T
    row = lax.broadcasted_iota(jnp.int32, (AUG_DIM - ATT_HEAD_DIM, tl), 0)
    for h in range(ATT_HEADS):
        qt_ref[h, 0:ATT_HEAD_DIM, :] = qnt[h * ATT_HEAD_DIM:(h + 1) * ATT_HEAD_DIM, :].astype(bf16)
        lo = BIAS_PIECES * (h % Q_PER_KV)
        qt_ref[h, ATT_HEAD_DIM:AUG_DIM, :] = jnp.where((row >= lo) & (row < lo + BIAS_PIECES), 1.0, 0.0).astype(bf16)

    kv = kv_ref[...]
    k = kv[:, 0:KV_DIM]
    msk = _dot2_lhs(k * k, seg[0:KV_DIM, :])
    fack = _dot2_lhs(lax.rsqrt(msk + RMS_EPS), segt[:, 0:KV_DIM])
    kn = k * fack * kw_ref[...]
    ko_ref[...] = kn
    v = kv[:, KV_DIM:2 * KV_DIM]
    vo_ref[...] = v
    vt_ref[...] = v.T.astype(bf16)
    dtp_ref[...] = kv[:, 2 * KV_DIM:2 * KV_DIM + 128]
    f_raw = kv[:, 2 * KV_DIM + SSD_HEADS:2 * KV_DIM + SSD_HEADS + ATT_HEADS]
    lf = jax.nn.log_sigmoid(f_raw + bf_ref[...])
    lf_ref[...] = lf

    tril = tril_ref[...]
    p0, p1, p2 = _split3(lf)
    c = _dot(tril, p0) + _dot(tril, p1) + _dot(tril, p2) + carry_ref[...]
    carry_ref[...] = c[tl - 1:tl, :]
    n0, n1, n2 = _split3(-c)
    ka = _dot(kn.astype(bf16), pk_ref[...]) + _dot(n0, pc_ref[0]) + _dot(n1, pc_ref[1]) + _dot(n2, pc_ref[2])
    for g in range(KV_HEADS):
        ka_ref[g] = ka[:, g * AUG_DIM:(g + 1) * AUG_DIM].astype(bf16)


def _qkv_post(main3, qw, kw, b_f, tl):
    bsz, seqlen, _ = main3.shape
    seg = np.zeros((ATT_DIM, 128), np.float32)
    for h in range(ATT_HEADS):
        seg[h * ATT_HEAD_DIM:(h + 1) * ATT_HEAD_DIM, h] = 1.0 / ATT_HEAD_DIM
    segt = (seg.T > 0).astype(np.float32)
    tril = np.tril(np.ones((tl, tl), np.float32))
    pk = np.zeros((KV_DIM, KV_HEADS * AUG_DIM), np.float32)
    pc = np.zeros((BIAS_PIECES, ATT_HEADS, KV_HEADS * AUG_DIM), np.float32)
    for g in range(KV_HEADS):
        for d in range(ATT_HEAD_DIM):
            pk[g * ATT_HEAD_DIM + d, g * AUG_DIM + d] = 1.0
        for j in range(Q_PER_KV):
            for i in range(BIAS_PIECES):
                pc[i, g * Q_PER_KV + j, g * AUG_DIM + ATT_HEAD_DIM + BIAS_PIECES * j + i] = 1.0
    full = lambda shp: pl.BlockSpec(shp, lambda b, i: tuple(0 for _ in shp))
    outs = pl.pallas_call(
        _qkv_body,
        grid=(bsz, seqlen // tl),
        in_specs=[
            pl.BlockSpec((None, tl, 1024), lambda b, i: (b, i, COL_Q // 1024)),
            pl.BlockSpec((None, tl, 1024), lambda b, i: (b, i, COL_KV // 1024)),
            full((1, ATT_DIM)), full((1, KV_DIM)), full((1, ATT_HEADS)),
            full((ATT_DIM, 128)), full((128, ATT_DIM)), full((tl, tl)),
            full((KV_DIM, KV_HEADS * AUG_DIM)), full((BIAS_PIECES, ATT_HEADS, KV_HEADS * AUG_DIM)),
        ],
        out_specs=[
            pl.BlockSpec((None, ATT_HEADS, AUG_DIM, tl), lambda b, i: (b, 0, 0, i)),
            pl.BlockSpec((None, KV_HEADS, tl, AUG_DIM), lambda b, i: (b, 0, i, 0)),
            pl.BlockSpec((None, KV_DIM, tl), lambda b, i: (b, 0, i)),
            pl.BlockSpec((None, tl, KV_DIM), lambda b, i: (b, i, 0)),
            pl.BlockSpec((None, tl, KV_DIM), lambda b, i: (b, i, 0)),
            pl.BlockSpec((None, tl, ATT_HEADS), lambda b, i: (b, i, 0)),
            pl.BlockSpec((None, tl, 128), lambda b, i: (b, i, 0)),
        ],
        out_shape=[
            jax.ShapeDtypeStruct((bsz, ATT_HEADS, AUG_DIM, seqlen), bf16),
            jax.ShapeDtypeStruct((bsz, KV_HEADS, seqlen, AUG_DIM), bf16),
            jax.ShapeDtypeStruct((bsz, KV_DIM, seqlen), bf16),
            jax.ShapeDtypeStruct((bsz, seqlen, KV_DIM), f32),
            jax.ShapeDtypeStruct((bsz, seqlen, KV_DIM), f32),
            jax.ShapeDtypeStruct((bsz, seqlen, ATT_HEADS), f32),
            jax.ShapeDtypeStruct((bsz, seqlen, 128), f32),
        ],
        scratch_shapes=[pltpu.VMEM((1, ATT_HEADS), f32)],
        compiler_params=_cparams(("parallel", "arbitrary")),
        name="qkv_post",
    )(main3, main3, qw, kw, b_f, jnp.asarray(seg, bf16), jnp.asarray(segt, bf16), jnp.asarray(tril, bf16),
      jnp.asarray(pk, bf16), jnp.asarray(pc, bf16))
    return outs


def _ssd_body(xbc_ref, z_ref, dtp_ref, prev_ref, h0_ref, cw_ref, cb_ref, dtb_ref, alog_ref, dskip_ref,
              nrm_ref, e2_ref, tril_ref, triu_ref, y_ref, hnew_ref, cnew_ref, xpad_ref, hs_ref, *, qr, q):
    c = pl.program_id(1)
    last = pl.num_programs(1) - 1

    @pl.when(c == 0)
    def _():
        xpad_ref[0:8, :] = jnp.zeros((8, CONV_DIM), f32)
        xpad_ref[8 - (D_CONV - 1):8, :] = prev_ref[...]
        hs_ref[...] = h0_ref[...]

    x = xbc_ref[...]
    xpad_ref[8:8 + qr, :] = x
    xa = xpad_ref[...]
    conv = cb_ref[...] + cw_ref[D_CONV - 1:D_CONV, :] * x
    for j in range(D_CONV - 1):
        shifted = pltpu.roll(xa, D_CONV - 1 - j, axis=0)[8:8 + qr, :]
        conv = conv + cw_ref[j:j + 1, :] * shifted
    xc = conv * jax.nn.sigmoid(conv)

    @pl.when(c == last)
    def _():
        cnew_ref[...] = xpad_ref[qr + 5:qr + 8, :]

    xpad_ref[0:8, :] = xpad_ref[qr:qr + 8, :]

    dt_all = jax.nn.softplus(dtp_ref[...][:, 0:SSD_HEADS] + dtb_ref[...])
    zz_all = z_ref[...]
    if q > qr:
        xc = jnp.concatenate([xc, jnp.zeros((q - qr, CONV_DIM), f32)], axis=0)
        dt_all = jnp.concatenate([dt_all, jnp.zeros((q - qr, SSD_HEADS), f32)], axis=0)
        zz_all = jnp.concatenate([zz_all, jnp.zeros((q - qr, D_SSM), f32)], axis=0)
    for sub in range(max(1, qr // q)):
        rows_in = slice(sub * q, (sub + 1) * q)
        nout = min(q, qr)
        y_ref[sub * q:sub * q + nout, :] = _ssd_chunk(
            xc[rows_in], dt_all[rows_in], zz_all[rows_in], hs_ref, alog_ref, dskip_ref, nrm_ref, e2_ref,
            tril_ref, triu_ref, q)[0:nout, :].astype(bf16)

    @pl.when(c == last)
    def _():
        hnew_ref[...] = hs_ref[...]


def _ssd_chunk(xc, dt, zz, hs_ref, alog_ref, dskip_ref, nrm_ref, e2_ref, tril_ref, triu_ref, q):
    xs = xc[:, 0:D_SSM]
    bm = xc[:, D_SSM:D_SSM + SSD_GROUPS * D_STATE].astype(bf16)
    cm = xc[:, D_SSM + SSD_GROUPS * D_STATE:CONV_DIM].astype(bf16)

    a = dt * (-jnp.exp(alog_ref[...]))
    tril = tril_ref[...]
    a0, a1, a2 = _split3(a)
    acs = _dot(tril, a0) + _dot(tril, a1) + _dot(tril, a2)
    triu = triu_ref[...]
    acst = _dot_tn(a0, triu) + _dot_tn(a1, triu) + _dot_tn(a2, triu)
    acs_last = acs[q - 1:q, :]
    fac = jnp.concatenate([dt, jnp.exp(acs_last - acs), jnp.exp(acs)], axis=0)
    fac_e = _dot(jnp.concatenate(_split2(fac), axis=1), e2_ref[...])
    xdt = xs * fac_e[0:q]
    xdtw = (xdt * fac_e[q:2 * q]).astype(bf16)
    eacs_e = fac_e[2 * q:3 * q]
    edec = jnp.exp(acst[:, q - 1:q])

    li = lax.broadcasted_iota(jnp.int32, (q, q), 0)
    si = lax.broadcasted_iota(jnp.int32, (q, q), 1)
    causal = li >= si
    lane = lax.broadcasted_iota(jnp.int32, (q, 2 * SSD_HEAD_DIM), 1)
    left = lane < SSD_HEAD_DIM

    y_groups = []
    for g in range(SSD_GROUPS):
        bg = bm[:, g * D_STATE:(g + 1) * D_STATE]
        cg = cm[:, g * D_STATE:(g + 1) * D_STATE]
        cbm = _dot_nt(cg, bg)
        r0 = g * HEADS_PER_GROUP * SSD_HEAD_DIM
        hg = hs_ref[r0:r0 + HEADS_PER_GROUP * SSD_HEAD_DIM, :].astype(bf16)
        y_off = _dot_nt(cg, hg)
        st = _dot_tn(xdtw[:, r0:r0 + HEADS_PER_GROUP * SSD_HEAD_DIM], bg)
        for hh in range(HEADS_PER_GROUP):
            h = g * HEADS_PER_GROUP + hh
            rows = slice(h * SSD_HEAD_DIM, (h + 1) * SSD_HEAD_DIM)
            hs_ref[rows, :] = (hs_ref[rows, :] * edec[h:h + 1, :]
                               + st[hh * SSD_HEAD_DIM:(hh + 1) * SSD_HEAD_DIM, :])
        pairs = []
        for pr in range(HEADS_PER_GROUP // 2):
            h0i = g * HEADS_PER_GROUP + 2 * pr
            xpair = xdt[:, h0i * SSD_HEAD_DIM:(h0i + 2) * SSD_HEAD_DIM]
            acc = None
            for t in range(2):
                h = h0i + t
                sg = acs[:, h:h + 1] - acst[h:h + 1, :]
                m = (cbm * jnp.exp(jnp.where(causal, sg, NEG_BIG))).astype(bf16)
                xm = jnp.where(left if t == 0 else jnp.logical_not(left), xpair, 0.0).astype(bf16)
                d = _dot(m, xm)
                acc = d if acc is None else acc + d
            pairs.append(acc)
        y_diag = jnp.concatenate(pairs, axis=1)
        y_groups.append(y_diag + y_off * eacs_e[:, r0:r0 + HEADS_PER_GROUP * SSD_HEAD_DIM])
    y = jnp.concatenate(y_groups, axis=1) + dskip_ref[...] * xs

    yg = y * (zz * jax.nn.sigmoid(zz))
    gw = D_SSM // SSD_GROUPS
    outs = []
    for g in range(SSD_GROUPS):
        t = yg[:, g * gw:(g + 1) * gw]
        outs.append(t * lax.rsqrt(jnp.mean(t * t, axis=-1, keepdims=True) + RMS_EPS))
    return jnp.concatenate(outs, axis=1) * nrm_ref[...]


def _ssd(main3, dtp, conv_prev, h0, conv_w, conv_b, dt_bias, a_log, d_skip_e, ssd_norm, qr, q):
    bsz, seqlen, _ = main3.shape
    nc = seqlen // qr
    e = np.zeros((SSD_HEADS, D_SSM), np.float32)
    for h in range(SSD_HEADS):
        e[h, h * SSD_HEAD_DIM:(h + 1) * SSD_HEAD_DIM] = 1.0
    tril = np.tril(np.ones((q, q), np.float32))
    nst = SSD_HEADS * SSD_HEAD_DIM
    full = lambda shp: pl.BlockSpec(shp, lambda b, c: tuple(0 for _ in shp))
    y, hnew, cnew = pl.pallas_call(
        functools.partial(_ssd_body, qr=qr, q=q),
        grid=(bsz, nc),
        in_specs=[
            pl.BlockSpec((None, qr, CONV_DIM), lambda b, c: (b, c, COL_XBC // CONV_DIM)),
            pl.BlockSpec((None, qr, D_SSM), lambda b, c: (b, c, COL_Z // D_SSM)),
            pl.BlockSpec((None, qr, 128), lambda b, c: (b, c, 0)),
            pl.BlockSpec((None, D_CONV - 1, CONV_DIM), lambda b, c: (b, 0, 0)),
            pl.BlockSpec((None, nst, D_STATE), lambda b, c: (b, 0, 0)),
            full((D_CONV, CONV_DIM)), full((1, CONV_DIM)), full((1, SSD_HEADS)), full((1, SSD_HEADS)),
            full((1, D_SSM)), full((1, D_SSM)), full((2 * SSD_HEADS, D_SSM)),
            full((q, q)), full((q, q)),
        ],
        out_specs=[
            pl.BlockSpec((None, qr, D_SSM), lambda b, c: (b, c, 0)),
            pl.BlockSpec((None, nst, D_STATE), lambda b, c: (b, 0, 0)),
            pl.BlockSpec((None, D_CONV - 1, CONV_DIM), lambda b, c: (b, 0, 0)),
        ],
        out_shape=[
            jax.ShapeDtypeStruct((bsz, seqlen, D_SSM), bf16),
            jax.ShapeDtypeStruct((bsz, nst, D_STATE), f32),
            jax.ShapeDtypeStruct((bsz, D_CONV - 1, CONV_DIM), f32),
        ],
        scratch_shapes=[pltpu.VMEM((qr + 8, CONV_DIM), f32), pltpu.VMEM((nst, D_STATE), f32)],
        compiler_params=_cparams(("parallel", "arbitrary")),
        name="ssd",
    )(main3, main3, dtp, conv_prev, h0, conv_w, conv_b, dt_bias, a_log, d_skip_e, ssd_norm,
      jnp.asarray(np.concatenate([e, e], axis=0), bf16), jnp.asarray(tril, bf16), jnp.asarray(tril.T, bf16))
    return y, hnew, cnew


FLASH_Q_LANES = 256
FLASH_KEY_ROWS = 128
FLASH_LOOKAHEAD = 4


def _flash_body(qi_ref, ki_ref, qt_ref, ka_ref, vt_ref, o_ref, m_ref, l_ref, acc_ref, *, tb):
    t = pl.program_id(2)
    qi = qi_ref[t]
    ki = ki_ref[t]
    ql, kr = FLASH_Q_LANES, FLASH_KEY_ROWS

    @pl.when(ki == 0)
    def _():
        m_ref[...] = jnp.full_like(m_ref, NEG_BIG)
        l_ref[...] = jnp.zeros_like(l_ref)
        acc_ref[...] = jnp.zeros_like(acc_ref)

    def process(diagonal):
        key_i = lax.broadcasted_iota(jnp.int32, (kr, ql), 0)
        qry_i = lax.broadcasted_iota(jnp.int32, (kr, ql), 1)
        tiles = [(j, r, kb) for kb in range(tb // kr) for j in range(Q_PER_KV) for r in range(tb // ql)
                 if not (diagonal and kb * kr > r * ql + ql - 1)]

        def scores(tile):
            j, r, kb = tile
            return _dot(ka_ref[kb * kr:(kb + 1) * kr, :], qt_ref[j, :, r * ql:(r + 1) * ql])

        def finish(tile, st):
            j, r, kb = tile
            cols = slice(r * ql, (r + 1) * ql)
            if diagonal and kb * kr + kr - 1 > r * ql:
                st = jnp.where(key_i + kb * kr <= qry_i + r * ql, st, NEG_BIG)
            m_prev = m_ref[j, :, cols]
            m_new = jnp.maximum(m_prev, jnp.max(st, axis=0, keepdims=True))
            alpha = jnp.exp(m_prev - m_new)
            p = jnp.exp(st - m_new)
            l_ref[j, :, cols] = alpha * l_ref[j, :, cols] + jnp.sum(p, axis=0, keepdims=True)
            acc_ref[j, :, cols] = (alpha * acc_ref[j, :, cols]
                                   + _dot(vt_ref[:, kb * kr:(kb + 1) * kr], p.astype(bf16)))
            m_ref[j, :, cols] = m_new

        pending = [scores(tl_) for tl_ in tiles[:FLASH_LOOKAHEAD]]
        for i, tile in enumerate(tiles):
            if i + FLASH_LOOKAHEAD < len(tiles):
                pending.append(scores(tiles[i + FLASH_LOOKAHEAD]))
            finish(tile, pending.pop(0))

    pl.when(ki < qi)(lambda: process(False))

    @pl.when(ki == qi)
    def _():
        process(True)
        o = acc_ref[...] / l_ref[...]
        o_ref[...] = jnp.concatenate([o[j].T for j in range(Q_PER_KV)], axis=1).astype(bf16)


def _flash_fox(qt, ka, vt, tb):
    bsz, _, _, seqlen = qt.shape
    nb = seqlen // tb
    qi_tab = np.array([i for i in range(nb) for j in range(i + 1)], np.int32)
    ki_tab = np.array([j for i in range(nb) for j in range(i + 1)], np.int32)
    grid_spec = pltpu.PrefetchScalarGridSpec(
        num_scalar_prefetch=2,
        grid=(bsz, KV_HEADS, qi_tab.shape[0]),
        in_specs=[
            pl.BlockSpec((None, Q_PER_KV, AUG_DIM, tb), lambda b, g, t, qi, ki: (b, g, 0, qi[t])),
            pl.BlockSpec((None, None, tb, AUG_DIM), lambda b, g, t, qi, ki: (b, g, ki[t], 0)),
            pl.BlockSpec((None, None, ATT_HEAD_DIM, tb), lambda b, g, t, qi, ki: (b, g, 0, ki[t])),
        ],
        out_specs=pl.BlockSpec((None, tb, Q_PER_KV * ATT_HEAD_DIM), lambda b, g, t, qi, ki: (b, qi[t], g)),
        scratch_shapes=[
            pltpu.VMEM((Q_PER_KV, 1, tb), f32),
            pltpu.VMEM((Q_PER_KV, 1, tb), f32),
            pltpu.VMEM((Q_PER_KV, ATT_HEAD_DIM, tb), f32),
        ],
    )
    return pl.pallas_call(
        functools.partial(_flash_body, tb=tb),
        grid_spec=grid_spec,
        out_shape=jax.ShapeDtypeStruct((bsz, seqlen, ATT_DIM), bf16),
        compiler_params=_cparams(("parallel", "parallel", "arbitrary")),
        name="flash_fox",
    )(jnp.asarray(qi_tab), jnp.asarray(ki_tab), qt, ka, vt)


PAGES_PER_STEP = 32


def _lane_cumsum(x):
    n = x.shape[1]
    lane = lax.broadcasted_iota(jnp.int32, x.shape, 1)
    sh = 1
    while sh < n:
        x = x + jnp.where(lane >= sh, pltpu.roll(x, sh, axis=1), 0.0)
        sh *= 2
    return x


PAGE_SLOTS = 3


def _paged_body(pt_ref, qbd_ref, kn_ref, vn_ref, lfn_ref, ck_hbm, cv_hbm, clf_hbm, o_ref,
                kbuf, vbuf, lfbuf, sem, m_ref, l_ref, acc_ref, carry_ref, *, dec_seq):
    npg = PAGES_PER_STEP
    b = pl.program_id(0)
    j = pl.program_id(1)
    nj = pl.num_programs(1)
    step = b * nj + j
    n_steps = pl.num_programs(0) * nj
    rows = qbd_ref.shape[0]

    def page_copies(s):
        sb = s // nj
        sj = s - sb * nj
        slot = s % PAGE_SLOTS
        for i in range(npg):
            pid = pt_ref[sb, sj * npg + i]
            yield pltpu.make_async_copy(ck_hbm.at[pid], kbuf.at[slot, i], sem.at[0, slot])
            yield pltpu.make_async_copy(cv_hbm.at[pid], vbuf.at[slot, i], sem.at[1, slot])
            yield pltpu.make_async_copy(clf_hbm.at[pid], lfbuf.at[slot, i], sem.at[2, slot])

    @pl.when(step == 0)
    def _():
        for s in range(PAGE_SLOTS - 1):
            for cp in page_copies(s):
                cp.start()

    @pl.when(j == 0)
    def _():
        m_ref[...] = jnp.full_like(m_ref, NEG_BIG)
        l_ref[...] = jnp.zeros_like(l_ref)
        acc_ref[...] = jnp.zeros_like(acc_ref)
        carry_ref[...] = jnp.zeros_like(carry_ref)

    for cp in page_copies(step):
        cp.wait()
    slot = step % PAGE_SLOTS

    qbd = qbd_ref[...]

    def attend(kt, vt, lft, mask):
        nkeys = kt.shape[1]
        ct = _lane_cumsum(lft) + carry_ref[...]
        carry_ref[...] = ct[:, nkeys - 1:nkeys]
        s = _dot(qbd, kt.astype(bf16))
        s3 = s.reshape(ATT_HEADS, rows // ATT_HEADS, nkeys) - ct[:, None, :]
        s = s3.reshape(rows, nkeys)
        if mask is not None:
            s = jnp.where(mask, s, NEG_BIG)
        m_prev = m_ref[...]
        m_new = jnp.maximum(m_prev, jnp.max(s, axis=-1, keepdims=True))
        alpha = jnp.exp(m_prev - m_new)
        p = jnp.exp(s - m_new)
        l_ref[...] = alpha * l_ref[...] + jnp.sum(p, axis=-1, keepdims=True)
        acc_ref[...] = alpha * acc_ref[...] + _dot_nt(p.astype(bf16), vt.astype(bf16))
        m_ref[...] = m_new

    kt = jnp.concatenate([kbuf[slot, i].reshape(KV_DIM, -1) for i in range(npg)], axis=1)
    vt = jnp.concatenate([vbuf[slot, i].reshape(KV_DIM, -1) for i in range(npg)], axis=1)
    lft = jnp.concatenate([lfbuf[slot, i] for i in range(npg)], axis=1)
    attend(kt, vt, lft, None)

    @pl.when(j == nj - 1)
    def _():
        nk = kn_ref.shape[1]
        r = lax.broadcasted_iota(jnp.int32, (rows, nk), 0) % dec_seq
        t = lax.broadcasted_iota(jnp.int32, (rows, nk), 1)
        attend(kn_ref[...], vn_ref[...], lfn_ref[...], t <= r)
        o_ref[...] = acc_ref[...] / l_ref[...]

    @pl.when(step + (PAGE_SLOTS - 1) < n_steps)
    def _():
        for cp in page_copies(step + (PAGE_SLOTS - 1)):
            cp.start()


def _paged_fox(page_table, qbd, knt, vnt, lfnt, ckt, cvt, clft, dec_seq):
    bsz, n_pages = page_table.shape
    rows = qbd.shape[1]
    page = ckt.shape[3]
    npg = PAGES_PER_STEP
    assert n_pages % npg == 0 and bsz * (n_pages // npg) >= PAGE_SLOTS - 1
    bmap = lambda b, j, pt: (b, 0, 0)
    hbm = pl.BlockSpec(memory_space=pl.ANY)
    grid_spec = pltpu.PrefetchScalarGridSpec(
        num_scalar_prefetch=1,
        grid=(bsz, n_pages // npg),
        in_specs=[
            pl.BlockSpec((None, rows, KV_DIM), bmap),
            pl.BlockSpec((None, KV_DIM, knt.shape[2]), bmap),
            pl.BlockSpec((None, KV_DIM, vnt.shape[2]), bmap),
            pl.BlockSpec((None, ATT_HEADS, lfnt.shape[2]), bmap),
            hbm, hbm, hbm,
        ],
        out_specs=pl.BlockSpec((None, rows, KV_DIM), bmap),
        scratch_shapes=[
            pltpu.VMEM((PAGE_SLOTS, npg, KV_HEADS, ATT_HEAD_DIM, page), f32),
            pltpu.VMEM((PAGE_SLOTS, npg, KV_HEADS, ATT_HEAD_DIM, page), f32),
            pltpu.VMEM((PAGE_SLOTS, npg, ATT_HEADS, page), f32),
            pltpu.SemaphoreType.DMA((3, PAGE_SLOTS)),
            pltpu.VMEM((rows, 1), f32), pltpu.VMEM((rows, 1), f32), pltpu.VMEM((rows, KV_DIM), f32),
            pltpu.VMEM((ATT_HEADS, 1), f32),
        ],
    )
    return pl.pallas_call(
        functools.partial(_paged_body, dec_seq=dec_seq),
        grid_spec=grid_spec,
        out_shape=jax.ShapeDtypeStruct((bsz, rows, KV_DIM), f32),
        compiler_params=_cparams(("arbitrary", "arbitrary")),
        name="paged_fox",
    )(page_table, qbd, knt, vnt, lfnt, ckt, cvt, clft)


def _merge_body(ys_ref, ya_ref, gs_ref, ga_ref, h_ref, ws_ref, wa_ref, wo_ref, o_ref):
    a = jax.nn.sigmoid(gs_ref[...]) * _dot(ys_ref[...], ws_ref[...])
    b = jax.nn.sigmoid(ga_ref[...]) * _dot(ya_ref[...], wa_ref[...])
    o_ref[...] = h_ref[...] + _dot((a + b).astype(bf16), wo_ref[...])


def _merge(y_ssd, y_att, main, h, ws, wa, wo, tm):
    n = h.shape[0]
    full = lambda shp: pl.BlockSpec(shp, lambda i: (0, 0))
    return pl.pallas_call(
        _merge_body,
        grid=(n // tm,),
        in_specs=[
            pl.BlockSpec((tm, D_SSM), lambda i: (i, 0)),
            pl.BlockSpec((tm, ATT_DIM), lambda i: (i, 0)),
            pl.BlockSpec((tm, D_MODEL), lambda i: (i, COL_GS // D_MODEL)),
            pl.BlockSpec((tm, D_MODEL), lambda i: (i, COL_GA // D_MODEL)),
            pl.BlockSpec((tm, D_MODEL), lambda i: (i, 0)),
            full((D_SSM, D_MODEL)), full((ATT_DIM, D_MODEL)), full((D_MODEL, D_MODEL)),
        ],
        out_specs=pl.BlockSpec((tm, D_MODEL), lambda i: (i, 0)),
        out_shape=jax.ShapeDtypeStruct((n, D_MODEL), f32),
        compiler_params=_cparams(("parallel",)),
        name="merge",
    )(y_ssd, y_att, main, main, h, ws, wa, wo)


def _pack_w_in(w_in):
    o = np.cumsum([0, D_SSM, CONV_DIM, SSD_HEADS, ATT_DIM, KV_DIM, KV_DIM, ATT_HEADS, D_MODEL, D_MODEL])
    z, xbc, dt, q, k, v, f, gs, ga = (w_in[:, int(o[i]):int(o[i + 1])] for i in range(9))
    pad = jnp.zeros((D_MODEL, COL_XBC - COL_KV - 2 * KV_DIM - SSD_HEADS - ATT_HEADS), w_in.dtype)
    w = jnp.concatenate([z, gs, ga, q, k, v, dt, f, pad, xbc], axis=1)
    return w.astype(bf16)


def _dense_pre(x2, p, tm):
    h, hn = _ffn(x2, p["ffn1_norm"], p["ffn1_wg"], p["ffn1_wu"], p["ffn1_wd"], tm, p["mix_norm"])
    main = _in_proj(hn, p["w_main"], min(x2.shape[0], 1024), W_MAIN_COLS // 4)
    return h, main


def _dense_post(y_ssd, y_att, main, h, p, tm):
    h2 = _merge(y_ssd, y_att, main, h, p["w_ssd_proj"], p["w_attn_proj"], p["w_out"], tm)
    return _ffn(h2, p["ffn2_norm"], p["ffn2_wg"], p["ffn2_wu"], p["ffn2_wd"], tm)


def _ssd_args(p):
    return (p["conv_w"], p["conv_b"], p["dt_bias"], p["a_log"], p["d_skip_e"], p["ssd_norm"])


def _prompt_layer(x, p):
    bsz, seqlen, _ = x.shape
    n = bsz * seqlen
    tm = 512
    h, main = _dense_pre(x.reshape(n, D_MODEL), p, tm)
    main3 = main.reshape(bsz, seqlen, W_MAIN_COLS)
    qt, ka, vt, k, v, logf, dtp = _qkv_post(main3, p["q_norm_e"], p["k_norm_e"], p["b_f"], 512)
    conv0 = jnp.zeros((bsz, D_CONV - 1, CONV_DIM), f32)
    h0 = jnp.zeros((bsz, SSD_HEADS * SSD_HEAD_DIM, D_STATE), f32)
    y_ssd, h_new, conv_new = _ssd(main3, dtp, conv0, h0, *_ssd_args(p), 2 * SSD_CHUNK, SSD_CHUNK)
    y_att = _flash_fox(qt, ka, vt.reshape(bsz, KV_HEADS, ATT_HEAD_DIM, seqlen), min(seqlen, 1024))
    y = _dense_post(y_ssd.reshape(n, D_SSM), y_att.reshape(n, ATT_DIM), main, h, p, tm)
    return (y.reshape(bsz, seqlen, D_MODEL),
            k.reshape(bsz, seqlen, KV_HEADS, ATT_HEAD_DIM), v.reshape(bsz, seqlen, KV_HEADS, ATT_HEAD_DIM),
            logf, h_new.reshape(bsz, SSD_HEADS, SSD_HEAD_DIM, D_STATE), conv_new)


def _sample_layer(x, conv_prev, h0, ck, cv, clf, page_table, p):
    bsz, t, _ = x.shape
    n = bsz * t
    tm = n
    h, main = _dense_pre(x.reshape(n, D_MODEL), p, tm)
    qt, _, _, k, v, logf, dtp = _qkv_post(main.reshape(1, n, W_MAIN_COLS), p["q_norm_e"], p["k_norm_e"], p["b_f"], n)
    main3 = main.reshape(bsz, t, W_MAIN_COLS)
    y_ssd, h_new, conv_new = _ssd(main3, dtp.reshape(bsz, t, 128), conv_prev,
                                  h0.reshape(bsz, SSD_HEADS * SSD_HEAD_DIM, D_STATE), *_ssd_args(p), t, 16)
    qg = qt[0, :, 0:ATT_HEAD_DIM, :].reshape(KV_HEADS, Q_PER_KV, ATT_HEAD_DIM, bsz, t).transpose(3, 0, 1, 4, 2)
    eye = jnp.eye(KV_HEADS, dtype=bf16)
    qbd = (qg[:, :, :, :, None, :] * eye[None, :, None, None, :, None]).reshape(bsz, ATT_HEADS * t, KV_DIM)
    page = ck.shape[1]
    padt = lambda a: jnp.pad(a.reshape(bsz, t, -1).transpose(0, 2, 1), ((0, 0), (0, 0), (0, page - t)))
    o = _paged_fox(page_table, qbd, padt(k), padt(v), padt(logf),
                   ck.transpose(0, 2, 3, 1), cv.transpose(0, 2, 3, 1), clf.transpose(0, 2, 1), t)
    o6 = o.reshape(bsz, KV_HEADS, Q_PER_KV, t, KV_HEADS, ATT_HEAD_DIM)
    y_att = jnp.stack([o6[:, g, :, :, g, :] for g in range(KV_HEADS)], axis=1)
    y_att = y_att.transpose(0, 3, 1, 2, 4).reshape(n, ATT_DIM).astype(bf16)
    y = _dense_post(y_ssd.reshape(n, D_SSM), y_att, main, h, p, tm)
    return (y.reshape(bsz, t, D_MODEL),
            k.reshape(bsz, t, KV_HEADS, ATT_HEAD_DIM), v.reshape(bsz, t, KV_HEADS, ATT_HEAD_DIM),
            logf.reshape(bsz, t, ATT_HEADS), h_new.reshape(bsz, SSD_HEADS, SSD_HEAD_DIM, D_STATE), conv_new)


def kernel(x_prompt, x_sample, cache_k, cache_v, cache_logf, state_ssm, state_conv, page_table, ffn1_norm, ffn1_w_gate, ffn1_w_up, ffn1_w_down, mix_norm, w_in, conv_w, conv_b, dt_bias, a_log, d_skip, ssd_norm, q_norm, k_norm, b_f, w_ssd_proj, w_attn_proj, w_out, ffn2_norm, ffn2_w_gate, ffn2_w_up, ffn2_w_down):
    depth = w_in.shape[0]
    yp, ys = x_prompt, x_sample
    prompt_states, sample_states = [], []
    for l in range(depth):
        p = {
            "ffn1_norm": ffn1_norm[l][None], "ffn1_wg": ffn1_w_gate[l].astype(bf16),
            "ffn1_wu": ffn1_w_up[l].astype(bf16), "ffn1_wd": ffn1_w_down[l].astype(bf16),
            "mix_norm": mix_norm[l][None], "w_main": _pack_w_in(w_in[l]),
            "conv_w": conv_w[l], "conv_b": conv_b[l][None], "dt_bias": dt_bias[l][None], "a_log": a_log[l][None],
            "d_skip_e": jnp.repeat(d_skip[l], SSD_HEAD_DIM)[None], "ssd_norm": ssd_norm[l][None],
            "q_norm_e": (jnp.tile(q_norm[l], ATT_HEADS) * (ATT_HEAD_DIM ** -0.5))[None], "k_norm_e": jnp.tile(k_norm[l], KV_HEADS)[None],
            "b_f": b_f[l][None],
            "w_ssd_proj": w_ssd_proj[l].astype(bf16), "w_attn_proj": w_attn_proj[l].astype(bf16),
            "w_out": w_out[l].astype(bf16),
            "ffn2_norm": ffn2_norm[l][None], "ffn2_wg": ffn2_w_gate[l].astype(bf16),
            "ffn2_wu": ffn2_w_up[l].astype(bf16), "ffn2_wd": ffn2_w_down[l].astype(bf16),
        }
        yp, *sp = _prompt_layer(yp, p)
        ys, *ss = _sample_layer(ys, state_conv[l], state_ssm[l], cache_k[l], cache_v[l], cache_logf[l],
                                page_table, p)
        prompt_states.append(sp)
        sample_states.append(ss)
    k_p, v_p, logf_p, ssm_p, conv_p = (jnp.stack(t) for t in zip(*prompt_states))
    k_s, v_s, logf_s, ssm_s, conv_s = (jnp.stack(t) for t in zip(*sample_states))
    return (yp, ys, k_p, v_p, logf_p, ssm_p, conv_p, k_s, v_s, logf_s, ssm_s, conv_s)
```

```python
import functools
import numpy as np
import jax
import jax.numpy as jnp
from jax import lax
from jax.experimental import pallas as pl
from jax.experimental.pallas import tpu as pltpu

f32 = jnp.float32
bf16 = jnp.bfloat16

RMS_EPS = 1e-6
D_MODEL = 1024
D_FF = 2816
D_SSM = 2048
SSD_HEAD_DIM = 64
SSD_HEADS = 32
SSD_GROUPS = 4
HEADS_PER_GROUP = SSD_HEADS // SSD_GROUPS
D_STATE = 128
D_CONV = 4
CONV_DIM = D_SSM + 2 * SSD_GROUPS * D_STATE
SSD_CHUNK = 128
ATT_HEAD_DIM = 64
ATT_HEADS = 16
KV_HEADS = 4
Q_PER_KV = ATT_HEADS // KV_HEADS
ATT_DIM = ATT_HEADS * ATT_HEAD_DIM
KV_DIM = KV_HEADS * ATT_HEAD_DIM
NEG_BIG = -1e30

COL_Z = 0
COL_GS = 2048
COL_GA = 3072
COL_Q = 4096
COL_KV = 5120
COL_XBC = 6144
W_MAIN_COLS = 9216

VMEM_LIMIT = 56 * 1024 * 1024


def _cparams(sem):
    return pltpu.CompilerParams(dimension_semantics=sem, vmem_limit_bytes=VMEM_LIMIT)


def _split2(x):
    hi = x.astype(bf16)
    lo = (x - hi.astype(f32)).astype(bf16)
    return hi, lo


def _split3(x):
    hi = x.astype(bf16)
    r = x - hi.astype(f32)
    mid = r.astype(bf16)
    lo = (r - mid.astype(f32)).astype(bf16)
    return hi, mid, lo


def _dot(a, b):
    return jnp.dot(a, b, preferred_element_type=f32)


def _dot_nt(a, b):
    return lax.dot_general(a, b, (((1,), (1,)), ((), ())), preferred_element_type=f32)


def _dot_tn(a, b):
    return lax.dot_general(a, b, (((0,), (0,)), ((), ())), preferred_element_type=f32)


def _dot2_lhs(x, m):
    hi, lo = _split2(x)
    return _dot(hi, m) + _dot(lo, m)


FFN_CHUNK = 256


def _rms(x, w):
    return x * lax.rsqrt(jnp.mean(x * x, axis=-1, keepdims=True) + RMS_EPS) * w


def _ffn_body(x_ref, nw_ref, wg_ref, wu_ref, wd_ref, nw2_ref, o_ref, *rest, emit_norm):
    a_ref = rest[-1]
    xn = _rms(x_ref[...], nw_ref[...]).astype(bf16)
    for c in range(D_FF // FFN_CHUNK):
        cols = slice(c * FFN_CHUNK, (c + 1) * FFN_CHUNK)
        g = _dot(xn, wg_ref[:, cols])
        u = _dot(xn, wu_ref[:, cols])
        a_ref[:, cols] = ((g * jax.nn.sigmoid(g)) * u).astype(bf16)
    y = x_ref[...] + 0.5 * _dot(a_ref[...], wd_ref[...])
    o_ref[...] = y
    if emit_norm:
        rest[0][...] = _rms(y, nw2_ref[...]).astype(bf16)


def _ffn(x, nw, wg, wu, wd, tm, nw2=None):
    n = x.shape[0]
    emit_norm = nw2 is not None
    full = lambda shp: pl.BlockSpec(shp, lambda i: (0, 0))
    row = pl.BlockSpec((tm, D_MODEL), lambda i: (i, 0))
    out_shape = [jax.ShapeDtypeStruct((n, D_MODEL), f32)]
    if emit_norm:
        out_shape.append(jax.ShapeDtypeStruct((n, D_MODEL), bf16))
    outs = pl.pallas_call(
        functools.partial(_ffn_body, emit_norm=emit_norm),
        grid=(n // tm,),
        in_specs=[row, full((1, D_MODEL)), full((D_MODEL, D_FF)), full((D_MODEL, D_FF)), full((D_FF, D_MODEL)),
                  full((1, D_MODEL))],
        out_specs=[row] * len(out_shape),
        out_shape=out_shape,
        scratch_shapes=[pltpu.VMEM((tm, D_FF), bf16)],
        compiler_params=_cparams(("parallel",)),
        name="ffn",
    )(x, nw, wg, wu, wd, nw2 if emit_norm else nw)
    return outs if emit_norm else outs[0]


def _proj_body(x_ref, wt_ref, o_ref):
    o_ref[...] = _dot_nt(x_ref[...], wt_ref[...])


def _in_proj(xn, wt, tm, tn):
    n = xn.shape[0]
    c = wt.shape[0]
    return pl.pallas_call(
        _proj_body,
        grid=(c // tn, n // tm),
        in_specs=[
            pl.BlockSpec((tm, D_MODEL), lambda j, i: (i, 0)),
            pl.BlockSpec((tn, D_MODEL), lambda j, i: (j, 0)),
        ],
        out_specs=pl.BlockSpec((tm, tn), lambda j, i: (i, j)),
        out_shape=jax.ShapeDtypeStruct((n, c), f32),
        compiler_params=_cparams(("parallel", "parallel")),
        name="in_proj",
    )(xn, wt)


AUG_DIM = 128
BIAS_PIECES = 3


def _qkv_body(q_ref, kv_ref, qw_ref, kw_ref, bf_ref, seg_ref, segt_ref, tril_ref, pk_ref, pc_ref,
              qt_ref, ka_ref, vt_ref, ko_ref, vo_ref, lf_ref, dtp_ref, carry_ref):
    @pl.when(pl.program_id(1) == 0)
    def _():
        carry_ref[...] = jnp.zeros_like(carry_ref)

    seg = seg_ref[...]
    segt = segt_ref[...]
    tl = q_ref.shape[0]

    q = q_ref[...]
    msq = _dot2_lhs(q * q, seg)
    fac = _dot2_lhs(lax.rsqrt(msq + RMS_EPS), segt)
    qnt = (q * fac * qw_ref[...]).T
    row = lax.broadcasted_iota(jnp.int32, (AUG_DIM - ATT_HEAD_DIM, tl), 0)
    for h in range(ATT_HEADS):
        qt_ref[h, 0:ATT_HEAD_DIM, :] = qnt[h * ATT_HEAD_DIM:(h + 1) * ATT_HEAD_DIM, :].astype(bf16)
        lo = BIAS_PIECES * (h % Q_PER_KV)
        qt_ref[h, ATT_HEAD_DIM:AUG_DIM, :] = jnp.where((row >= lo) & (row < lo + BIAS_PIECES), 1.0, 0.0).astype(bf16)

    kv = kv_ref[...]
    k = kv[:, 0:KV_DIM]
    msk = _dot2_lhs(k * k, seg[0:KV_DIM, :])
    fack = _dot2_lhs(lax.rsqrt(msk + RMS_EPS), segt[:, 0:KV_DIM])
    kn = k * fack * kw_ref[...]
    ko_ref[...] = kn
    v = kv[:, KV_DIM:2 * KV_DIM]
    vo_ref[...] = v
    vt_ref[...] = v.T.astype(bf16)
    dtp_ref[...] = kv[:, 2 * KV_DIM:2 * KV_DIM + 128]
    f_raw = kv[:, 2 * KV_DIM + SSD_HEADS:2 * KV_DIM + SSD_HEADS + ATT_HEADS]
    lf = jax.nn.log_sigmoid(f_raw + bf_ref[...])
    lf_ref[...] = lf

    tril = tril_ref[...]
    p0, p1, p2 = _split3(lf)
    c = _dot(tril, p0) + _dot(tril, p1) + _dot(tril, p2) + carry_ref[...]
    carry_ref[...] = c[tl - 1:tl, :]
    n0, n1, n2 = _split3(-c)
    ka = _dot(kn.astype(bf16), pk_ref[...]) + _dot(n0, pc_ref[0]) + _dot(n1, pc_ref[1]) + _dot(n2, pc_ref[2])
    for g in range(KV_HEADS):
        ka_ref[g] = ka[:, g * AUG_DIM:(g + 1) * AUG_DIM].astype(bf16)


def _qkv_post(main3, qw, kw, b_f, tl):
    bsz, seqlen, _ = main3.shape
    seg = np.zeros((ATT_DIM, 128), np.float32)
    for h in range(ATT_HEADS):
        seg[h * ATT_HEAD_DIM:(h + 1) * ATT_HEAD_DIM, h] = 1.0 / ATT_HEAD_DIM
    segt = (seg.T > 0).astype(np.float32)
    tril = np.tril(np.ones((tl, tl), np.float32))
    pk = np.zeros((KV_DIM, KV_HEADS * AUG_DIM), np.float32)
    pc = np.zeros((BIAS_PIECES, ATT_HEADS, KV_HEADS * AUG_DIM), np.float32)
    for g in range(KV_HEADS):
        for d in range(ATT_HEAD_DIM):
            pk[g * ATT_HEAD_DIM + d, g * AUG_DIM + d] = 1.0
        for j in range(Q_PER_KV):
            for i in range(BIAS_PIECES):
                pc[i, g * Q_PER_KV + j, g * AUG_DIM + ATT_HEAD_DIM + BIAS_PIECES * j + i] = 1.0
    full = lambda shp: pl.BlockSpec(shp, lambda b, i: tuple(0 for _ in shp))
    outs = pl.pallas_call(
        _qkv_body,
        grid=(bsz, seqlen // tl),
        in_specs=[
            pl.BlockSpec((None, tl, 1024), lambda b, i: (b, i, COL_Q // 1024)),
            pl.BlockSpec((None, tl, 1024), lambda b, i: (b, i, COL_KV // 1024)),
            full((1, ATT_DIM)), full((1, KV_DIM)), full((1, ATT_HEADS)),
            full((ATT_DIM, 128)), full((128, ATT_DIM)), full((tl, tl)),
            full((KV_DIM, KV_HEADS * AUG_DIM)), full((BIAS_PIECES, ATT_HEADS, KV_HEADS * AUG_DIM)),
        ],
        out_specs=[
            pl.BlockSpec((None, ATT_HEADS, AUG_DIM, tl), lambda b, i: (b, 0, 0, i)),
            pl.BlockSpec((None, KV_HEADS, tl, AUG_DIM), lambda b, i: (b, 0, i, 0)),
            pl.BlockSpec((None, KV_DIM, tl), lambda b, i: (b, 0, i)),
            pl.BlockSpec((None, tl, KV_DIM), lambda b, i: (b, i, 0)),
            pl.BlockSpec((None, tl, KV_DIM), lambda b, i: (b, i, 0)),
            pl.BlockSpec((None, tl, ATT_HEADS), lambda b, i: (b, i, 0)),
            pl.BlockSpec((None, tl, 128), lambda b, i: (b, i, 0)),
        ],
        out_shape=[
            jax.ShapeDtypeStruct((bsz, ATT_HEADS, AUG_DIM, seqlen), bf16),
            jax.ShapeDtypeStruct((bsz, KV_HEADS, seqlen, AUG_DIM), bf16),
            jax.ShapeDtypeStruct((bsz, KV_DIM, seqlen), bf16),
            jax.ShapeDtypeStruct((bsz, seqlen, KV_DIM), f32),
            jax.ShapeDtypeStruct((bsz, seqlen, KV_DIM), f32),
            jax.ShapeDtypeStruct((bsz, seqlen, ATT_HEADS), f32),
            jax.ShapeDtypeStruct((bsz, seqlen, 128), f32),
        ],
        scratch_shapes=[pltpu.VMEM((1, ATT_HEADS), f32)],
        compiler_params=_cparams(("parallel", "arbitrary")),
        name="qkv_post",
    )(main3, main3, qw, kw, b_f, jnp.asarray(seg, bf16), jnp.asarray(segt, bf16), jnp.asarray(tril, bf16),
      jnp.asarray(pk, bf16), jnp.asarray(pc, bf16))
    return outs


def _ssd_body(xbc_ref, z_ref, dtp_ref, prev_ref, h0_ref, cw_ref, cb_ref, dtb_ref, alog_ref, dskip_ref,
              nrm_ref, e2_ref, tril_ref, triu_ref, y_ref, hnew_ref, cnew_ref, xpad_ref, hs_ref, *, qr, q):
    c = pl.program_id(1)
    last = pl.num_programs(1) - 1

    @pl.when(c == 0)
    def _():
        xpad_ref[0:8, :] = jnp.zeros((8, CONV_DIM), f32)
        xpad_ref[8 - (D_CONV - 1):8, :] = prev_ref[...]
        hs_ref[...] = h0_ref[...]

    x = xbc_ref[...]
    xpad_ref[8:8 + qr, :] = x
    xa = xpad_ref[...]
    conv = cb_ref[...] + cw_ref[D_CONV - 1:D_CONV, :] * x
    for j in range(D_CONV - 1):
        shifted = pltpu.roll(xa, D_CONV - 1 - j, axis=0)[8:8 + qr, :]
        conv = conv + cw_ref[j:j + 1, :] * shifted
    xc = conv * jax.nn.sigmoid(conv)

    @pl.when(c == last)
    def _():
        cnew_ref[...] = xpad_ref[qr + 5:qr + 8, :]

    xpad_ref[0:8, :] = xpad_ref[qr:qr + 8, :]

    dt_all = jax.nn.softplus(dtp_ref[...][:, 0:SSD_HEADS] + dtb_ref[...])
    zz_all = z_ref[...]
    if q > qr:
        xc = jnp.concatenate([xc, jnp.zeros((q - qr, CONV_DIM), f32)], axis=0)
        dt_all = jnp.concatenate([dt_all, jnp.zeros((q - qr, SSD_HEADS), f32)], axis=0)
        zz_all = jnp.concatenate([zz_all, jnp.zeros((q - qr, D_SSM), f32)], axis=0)
    for sub in range(max(1, qr // q)):
        rows_in = slice(sub * q, (sub + 1) * q)
        nout = min(q, qr)
        y_ref[sub * q:sub * q + nout, :] = _ssd_chunk(
            xc[rows_in], dt_all[rows_in], zz_all[rows_in], hs_ref, alog_ref, dskip_ref, nrm_ref, e2_ref,
            tril_ref, triu_ref, q)[0:nout, :].astype(bf16)

    @pl.when(c == last)
    def _():
        hnew_ref[...] = hs_ref[...]


def _ssd_chunk(xc, dt, zz, hs_ref, alog_ref, dskip_ref, nrm_ref, e2_ref, tril_ref, triu_ref, q):
    xs = xc[:, 0:D_SSM]
    bm = xc[:, D_SSM:D_SSM + SSD_GROUPS * D_STATE].astype(bf16)
    cm = xc[:, D_SSM + SSD_GROUPS * D_STATE:CONV_DIM].astype(bf16)

    a = dt * (-jnp.exp(alog_ref[...]))
    tril = tril_ref[...]
    a0, a1, a2 = _split3(a)
    acs = _dot(tril, a0) + _dot(tril, a1) + _dot(tril, a2)
    triu = triu_ref[...]
    acst = _dot_tn(a0, triu) + _dot_tn(a1, triu) + _dot_tn(a2, triu)
    acs_last = acs[q - 1:q, :]
    fac = jnp.concatenate([dt, jnp.exp(acs_last - acs), jnp.exp(acs)], axis=0)
    fac_e = _dot(jnp.concatenate(_split2(fac), axis=1), e2_ref[...])
    xdt = xs * fac_e[0:q]
    xdtw = (xdt * fac_e[q:2 * q]).astype(bf16)
    eacs_e = fac_e[2 * q:3 * q]
    edec = jnp.exp(acst[:, q - 1:q])

    li = lax.broadcasted_iota(jnp.int32, (q, q), 0)
    si = lax.broadcasted_iota(jnp.int32, (q, q), 1)
    causal = li >= si
    lane = lax.broadcasted_iota(jnp.int32, (q, 2 * SSD_HEAD_DIM), 1)
    left = lane < SSD_HEAD_DIM

    y_groups = []
    for g in range(SSD_GROUPS):
        bg = bm[:, g * D_STATE:(g + 1) * D_STATE]
        cg = cm[:, g * D_STATE:(g + 1) * D_STATE]
        cbm = _dot_nt(cg, bg)
        r0 = g * HEADS_PER_GROUP * SSD_HEAD_DIM
        hg = hs_ref[r0:r0 + HEADS_PER_GROUP * SSD_HEAD_DIM, :].astype(bf16)
        y_off = _dot_nt(cg, hg)
        st = _dot_tn(xdtw[:, r0:r0 + HEADS_PER_GROUP * SSD_HEAD_DIM], bg)
        for hh in range(HEADS_PER_GROUP):
            h = g * HEADS_PER_GROUP + hh
            rows = slice(h * SSD_HEAD_DIM, (h + 1) * SSD_HEAD_DIM)
            hs_ref[rows, :] = (hs_ref[rows, :] * edec[h:h + 1, :]
                               + st[hh * SSD_HEAD_DIM:(hh + 1) * SSD_HEAD_DIM, :])
        pairs = []
        for pr in range(HEADS_PER_GROUP // 2):
            h0i = g * HEADS_PER_GROUP + 2 * pr
            xpair = xdt[:, h0i * SSD_HEAD_DIM:(h0i + 2) * SSD_HEAD_DIM]
            acc = None
            for t in range(2):
                h = h0i + t
                sg = acs[:, h:h + 1] - acst[h:h + 1, :]
                m = (cbm * jnp.exp(jnp.where(causal, sg, NEG_BIG))).astype(bf16)
                xm = jnp.where(left if t == 0 else jnp.logical_not(left), xpair, 0.0).astype(bf16)
                d = _dot(m, xm)
                acc = d if acc is None else acc + d
            pairs.append(acc)
        y_diag = jnp.concatenate(pairs, axis=1)
        y_groups.append(y_diag + y_off * eacs_e[:, r0:r0 + HEADS_PER_GROUP * SSD_HEAD_DIM])
    y = jnp.concatenate(y_groups, axis=1) + dskip_ref[...] * xs

    yg = y * (zz * jax.nn.sigmoid(zz))
    gw = D_SSM // SSD_GROUPS
    outs = []
    for g in range(SSD_GROUPS):
        t = yg[:, g * gw:(g + 1) * gw]
        outs.append(t * lax.rsqrt(jnp.mean(t * t, axis=-1, keepdims=True) + RMS_EPS))
    return jnp.concatenate(outs, axis=1) * nrm_ref[...]


def _ssd(main3, dtp, conv_prev, h0, conv_w, conv_b, dt_bias, a_log, d_skip_e, ssd_norm, qr, q):
    bsz, seqlen, _ = main3.shape
    nc = seqlen // qr
    e = np.zeros((SSD_HEADS, D_SSM), np.float32)
    for h in range(SSD_HEADS):
        e[h, h * SSD_HEAD_DIM:(h + 1) * SSD_HEAD_DIM] = 1.0
    tril = np.tril(np.ones((q, q), np.float32))
    nst = SSD_HEADS * SSD_HEAD_DIM
    full = lambda shp: pl.BlockSpec(shp, lambda b, c: tuple(0 for _ in shp))
    y, hnew, cnew = pl.pallas_call(
        functools.partial(_ssd_body, qr=qr, q=q),
        grid=(bsz, nc),
        in_specs=[
            pl.BlockSpec((None, qr, CONV_DIM), lambda b, c: (b, c, COL_XBC // CONV_DIM)),
            pl.BlockSpec((None, qr, D_SSM), lambda b, c: (b, c, COL_Z // D_SSM)),
            pl.BlockSpec((None, qr, 128), lambda b, c: (b, c, 0)),
            pl.BlockSpec((None, D_CONV - 1, CONV_DIM), lambda b, c: (b, 0, 0)),
            pl.BlockSpec((None, nst, D_STATE), lambda b, c: (b, 0, 0)),
            full((D_CONV, CONV_DIM)), full((1, CONV_DIM)), full((1, SSD_HEADS)), full((1, SSD_HEADS)),
            full((1, D_SSM)), full((1, D_SSM)), full((2 * SSD_HEADS, D_SSM)),
            full((q, q)), full((q, q)),
        ],
        out_specs=[
            pl.BlockSpec((None, qr, D_SSM), lambda b, c: (b, c, 0)),
            pl.BlockSpec((None, nst, D_STATE), lambda b, c: (b, 0, 0)),
            pl.BlockSpec((None, D_CONV - 1, CONV_DIM), lambda b, c: (b, 0, 0)),
        ],
        out_shape=[
            jax.ShapeDtypeStruct((bsz, seqlen, D_SSM), bf16),
            jax.ShapeDtypeStruct((bsz, nst, D_STATE), f32),
            jax.ShapeDtypeStruct((bsz, D_CONV - 1, CONV_DIM), f32),
        ],
        scratch_shapes=[pltpu.VMEM((qr + 8, CONV_DIM), f32), pltpu.VMEM((nst, D_STATE), f32)],
        compiler_params=_cparams(("parallel", "arbitrary")),
        name="ssd",
    )(main3, main3, dtp, conv_prev, h0, conv_w, conv_b, dt_bias, a_log, d_skip_e, ssd_norm,
      jnp.asarray(np.concatenate([e, e], axis=0), bf16), jnp.asarray(tril, bf16), jnp.asarray(tril.T, bf16))
    return y, hnew, cnew


FLASH_Q_LANES = 256
FLASH_KEY_ROWS = 128
FLASH_LOOKAHEAD = 6


def _flash_body(qi_ref, ki_ref, qt_ref, ka_ref, vt_ref, o_ref, m_ref, l_ref, acc_ref, *, tb):
    t = pl.program_id(2)
    qi = qi_ref[t]
    ki = ki_ref[t]
    ql, kr = FLASH_Q_LANES, FLASH_KEY_ROWS

    @pl.when(ki == 0)
    def _():
        m_ref[...] = jnp.full_like(m_ref, NEG_BIG)
        l_ref[...] = jnp.zeros_like(l_ref)
        acc_ref[...] = jnp.zeros_like(acc_ref)

    def process(diagonal):
        key_i = lax.broadcasted_iota(jnp.int32, (kr, ql), 0)
        qry_i = lax.broadcasted_iota(jnp.int32, (kr, ql), 1)
        tiles = [(j, r, kb) for kb in range(tb // kr) for j in range(Q_PER_KV) for r in range(tb // ql)
                 if not (diagonal and kb * kr > r * ql + ql - 1)]

        def scores(tile):
            j, r, kb = tile
            return _dot(ka_ref[kb * kr:(kb + 1) * kr, :], qt_ref[j, :, r * ql:(r + 1) * ql])

        def finish(tile, st):
            j, r, kb = tile
            cols = slice(r * ql, (r + 1) * ql)
            if diagonal and kb * kr + kr - 1 > r * ql:
                st = jnp.where(key_i + kb * kr <= qry_i + r * ql, st, NEG_BIG)
            st3 = st.reshape(kr // 8, 8, ql)
            mx = jnp.max(st3, axis=0)
            for sh in (4, 2, 1):
                mx = jnp.maximum(mx, pltpu.roll(mx, sh, axis=0))
            m_prev = m_ref[j, :, cols]
            m_new = jnp.maximum(m_prev, mx)
            alpha = jnp.exp(m_prev - m_new)
            p3 = jnp.exp(st3 - m_new[None])
            l_ref[j, :, cols] = alpha * l_ref[j, :, cols] + jnp.sum(p3, axis=0)
            pv = _dot(vt_ref[:, kb * kr:(kb + 1) * kr], p3.reshape(kr, ql).astype(bf16))
            acc3 = acc_ref[j, :, cols].reshape(ATT_HEAD_DIM // 8, 8, ql) * alpha[None]
            acc_ref[j, :, cols] = acc3.reshape(ATT_HEAD_DIM, ql) + pv
            m_ref[j, :, cols] = m_new

        pending = [scores(tl_) for tl_ in tiles[:FLASH_LOOKAHEAD]]
        for i, tile in enumerate(tiles):
            if i + FLASH_LOOKAHEAD < len(tiles):
                pending.append(scores(tiles[i + FLASH_LOOKAHEAD]))
            finish(tile, pending.pop(0))

    pl.when(ki < qi)(lambda: process(False))

    @pl.when(ki == qi)
    def _():
        process(True)
        o = acc_ref[...] / jnp.sum(l_ref[...], axis=1, keepdims=True)
        o_ref[...] = jnp.concatenate([o[j].T for j in range(Q_PER_KV)], axis=1).astype(bf16)


def _flash_fox(qt, ka, vt, tb):
    bsz, _, _, seqlen = qt.shape
    nb = seqlen // tb
    qi_tab = np.array([i for i in range(nb) for j in range(i + 1)], np.int32)
    ki_tab = np.array([j for i in range(nb) for j in range(i + 1)], np.int32)
    grid_spec = pltpu.PrefetchScalarGridSpec(
        num_scalar_prefetch=2,
        grid=(bsz, KV_HEADS, qi_tab.shape[0]),
        in_specs=[
            pl.BlockSpec((None, Q_PER_KV, AUG_DIM, tb), lambda b, g, t, qi, ki: (b, g, 0, qi[t])),
            pl.BlockSpec((None, None, tb, AUG_DIM), lambda b, g, t, qi, ki: (b, g, ki[t], 0)),
            pl.BlockSpec((None, None, ATT_HEAD_DIM, tb), lambda b, g, t, qi, ki: (b, g, 0, ki[t])),
        ],
        out_specs=pl.BlockSpec((None, tb, Q_PER_KV * ATT_HEAD_DIM), lambda b, g, t, qi, ki: (b, qi[t], g)),
        scratch_shapes=[
            pltpu.VMEM((Q_PER_KV, 8, tb), f32),
            pltpu.VMEM((Q_PER_KV, 8, tb), f32),
            pltpu.VMEM((Q_PER_KV, ATT_HEAD_DIM, tb), f32),
        ],
    )
    return pl.pallas_call(
        functools.partial(_flash_body, tb=tb),
        grid_spec=grid_spec,
        out_shape=jax.ShapeDtypeStruct((bsz, seqlen, ATT_DIM), bf16),
        compiler_params=_cparams(("parallel", "parallel", "arbitrary")),
        name="flash_fox",
    )(jnp.asarray(qi_tab), jnp.asarray(ki_tab), qt, ka, vt)


PAGES_PER_STEP = 32


def _lane_cumsum(x):
    n = x.shape[1]
    lane = lax.broadcasted_iota(jnp.int32, x.shape, 1)
    sh = 1
    while sh < n:
        x = x + jnp.where(lane >= sh, pltpu.roll(x, sh, axis=1), 0.0)
        sh *= 2
    return x


PAGE_SLOTS = 3


def _paged_body(pt_ref, qbd_ref, kn_ref, vn_ref, lfn_ref, ck_hbm, cv_hbm, clf_hbm, o_ref,
                kbuf, vbuf, lfbuf, sem, m_ref, l_ref, acc_ref, carry_ref, *, dec_seq):
    npg = PAGES_PER_STEP
    b = pl.program_id(0)
    j = pl.program_id(1)
    nj = pl.num_programs(1)
    step = b * nj + j
    n_steps = pl.num_programs(0) * nj
    rows = qbd_ref.shape[0]

    def page_copies(s):
        sb = s // nj
        sj = s - sb * nj
        slot = s % PAGE_SLOTS
        for i in range(npg):
            pid = pt_ref[sb, sj * npg + i]
            yield pltpu.make_async_copy(ck_hbm.at[pid], kbuf.at[slot, i], sem.at[0, slot])
            yield pltpu.make_async_copy(cv_hbm.at[pid], vbuf.at[slot, i], sem.at[1, slot])
            yield pltpu.make_async_copy(clf_hbm.at[pid], lfbuf.at[slot, i], sem.at[2, slot])

    @pl.when(step == 0)
    def _():
        for s in range(PAGE_SLOTS - 1):
            for cp in page_copies(s):
                cp.start()

    @pl.when(j == 0)
    def _():
        m_ref[...] = jnp.full_like(m_ref, NEG_BIG)
        l_ref[...] = jnp.zeros_like(l_ref)
        acc_ref[...] = jnp.zeros_like(acc_ref)
        carry_ref[...] = jnp.zeros_like(carry_ref)

    for cp in page_copies(step):
        cp.wait()
    slot = step % PAGE_SLOTS

    qbd = qbd_ref[...]

    def attend(kt, vt, lft, mask):
        nkeys = kt.shape[1]
        ct = _lane_cumsum(lft) + carry_ref[...]
        carry_ref[...] = ct[:, nkeys - 1:nkeys]
        s = _dot(qbd, kt.astype(bf16))
        s3 = s.reshape(ATT_HEADS, rows // ATT_HEADS, nkeys) - ct[:, None, :]
        s = s3.reshape(rows, nkeys)
        if mask is not None:
            s = jnp.where(mask, s, NEG_BIG)
        m_prev = m_ref[...]
        m_new = jnp.maximum(m_prev, jnp.max(s, axis=-1, keepdims=True))
        alpha = jnp.exp(m_prev - m_new)
        p = jnp.exp(s - m_new)
        l_ref[...] = alpha * l_ref[...] + jnp.sum(p, axis=-1, keepdims=True)
        acc_ref[...] = alpha * acc_ref[...] + _dot_nt(p.astype(bf16), vt.astype(bf16))
        m_ref[...] = m_new

    kt = jnp.concatenate([kbuf[slot, i].reshape(KV_DIM, -1) for i in range(npg)], axis=1)
    vt = jnp.concatenate([vbuf[slot, i].reshape(KV_DIM, -1) for i in range(npg)], axis=1)
    lft = jnp.concatenate([lfbuf[slot, i] for i in range(npg)], axis=1)
    attend(kt, vt, lft, None)

    @pl.when(j == nj - 1)
    def _():
        nk = kn_ref.shape[1]
        r = lax.broadcasted_iota(jnp.int32, (rows, nk), 0) % dec_seq
        t = lax.broadcasted_iota(jnp.int32, (rows, nk), 1)
        attend(kn_ref[...], vn_ref[...], lfn_ref[...], t <= r)
        o_ref[...] = acc_ref[...] / l_ref[...]

    @pl.when(step + (PAGE_SLOTS - 1) < n_steps)
    def _():
        for cp in page_copies(step + (PAGE_SLOTS - 1)):
            cp.start()


def _paged_fox(page_table, qbd, knt, vnt, lfnt, ckt, cvt, clft, dec_seq):
    bsz, n_pages = page_table.shape
    rows = qbd.shape[1]
    page = ckt.shape[3]
    npg = PAGES_PER_STEP
    assert n_pages % npg == 0 and bsz * (n_pages // npg) >= PAGE_SLOTS - 1
    bmap = lambda b, j, pt: (b, 0, 0)
    hbm = pl.BlockSpec(memory_space=pl.ANY)
    grid_spec = pltpu.PrefetchScalarGridSpec(
        num_scalar_prefetch=1,
        grid=(bsz, n_pages // npg),
        in_specs=[
            pl.BlockSpec((None, rows, KV_DIM), bmap),
            pl.BlockSpec((None, KV_DIM, knt.shape[2]), bmap),
            pl.BlockSpec((None, KV_DIM, vnt.shape[2]), bmap),
            pl.BlockSpec((None, ATT_HEADS, lfnt.shape[2]), bmap),
            hbm, hbm, hbm,
        ],
        out_specs=pl.BlockSpec((None, rows, KV_DIM), bmap),
        scratch_shapes=[
            pltpu.VMEM((PAGE_SLOTS, npg, KV_HEADS, ATT_HEAD_DIM, page), f32),
            pltpu.VMEM((PAGE_SLOTS, npg, KV_HEADS, ATT_HEAD_DIM, page), f32),
            pltpu.VMEM((PAGE_SLOTS, npg, ATT_HEADS, page), f32),
            pltpu.SemaphoreType.DMA((3, PAGE_SLOTS)),
            pltpu.VMEM((rows, 1), f32), pltpu.VMEM((rows, 1), f32), pltpu.VMEM((rows, KV_DIM), f32),
            pltpu.VMEM((ATT_HEADS, 1), f32),
        ],
    )
    return pl.pallas_call(
        functools.partial(_paged_body, dec_seq=dec_seq),
        grid_spec=grid_spec,
        out_shape=jax.ShapeDtypeStruct((bsz, rows, KV_DIM), f32),
        compiler_params=_cparams(("arbitrary", "arbitrary")),
        name="paged_fox",
    )(page_table, qbd, knt, vnt, lfnt, ckt, cvt, clft)


def _merge_body(ys_ref, ya_ref, gs_ref, ga_ref, h_ref, ws_ref, wa_ref, wo_ref, o_ref):
    a = jax.nn.sigmoid(gs_ref[...]) * _dot(ys_ref[...], ws_ref[...])
    b = jax.nn.sigmoid(ga_ref[...]) * _dot(ya_ref[...], wa_ref[...])
    o_ref[...] = h_ref[...] + _dot((a + b).astype(bf16), wo_ref[...])


def _merge(y_ssd, y_att, main, h, ws, wa, wo, tm):
    n = h.shape[0]
    full = lambda shp: pl.BlockSpec(shp, lambda i: (0, 0))
    return pl.pallas_call(
        _merge_body,
        grid=(n // tm,),
        in_specs=[
            pl.BlockSpec((tm, D_SSM), lambda i: (i, 0)),
            pl.BlockSpec((tm, ATT_DIM), lambda i: (i, 0)),
            pl.BlockSpec((tm, D_MODEL), lambda i: (i, COL_GS // D_MODEL)),
            pl.BlockSpec((tm, D_MODEL), lambda i: (i, COL_GA // D_MODEL)),
            pl.BlockSpec((tm, D_MODEL), lambda i: (i, 0)),
            full((D_SSM, D_MODEL)), full((ATT_DIM, D_MODEL)), full((D_MODEL, D_MODEL)),
        ],
        out_specs=pl.BlockSpec((tm, D_MODEL), lambda i: (i, 0)),
        out_shape=jax.ShapeDtypeStruct((n, D_MODEL), f32),
        compiler_params=_cparams(("parallel",)),
        name="merge",
    )(y_ssd, y_att, main, main, h, ws, wa, wo)


def _pack_w_in(w_in):
    o = np.cumsum([0, D_SSM, CONV_DIM, SSD_HEADS, ATT_DIM, KV_DIM, KV_DIM, ATT_HEADS, D_MODEL, D_MODEL])
    wt = w_in.T
    z, xbc, dt, q, k, v, f, gs, ga = (wt[int(o[i]):int(o[i + 1])] for i in range(9))
    pad = jnp.zeros((COL_XBC - COL_KV - 2 * KV_DIM - SSD_HEADS - ATT_HEADS, D_MODEL), w_in.dtype)
    return jnp.concatenate([z, gs, ga, q, k, v, dt, f, pad, xbc], axis=0).astype(bf16)


def _dense_pre(x2, p, tm):
    h, hn = _ffn(x2, p["ffn1_norm"], p["ffn1_wg"], p["ffn1_wu"], p["ffn1_wd"], tm, p["mix_norm"])
    main = _in_proj(hn, p["w_main"], min(x2.shape[0], 1024), W_MAIN_COLS // 4)
    return h, main


def _dense_post(y_ssd, y_att, main, h, p, tm):
    h2 = _merge(y_ssd, y_att, main, h, p["w_ssd_proj"], p["w_attn_proj"], p["w_out"], tm)
    return _ffn(h2, p["ffn2_norm"], p["ffn2_wg"], p["ffn2_wu"], p["ffn2_wd"], tm)


def _ssd_args(p):
    return (p["conv_w"], p["conv_b"], p["dt_bias"], p["a_log"], p["d_skip_e"], p["ssd_norm"])


def _prompt_layer(x, p):
    bsz, seqlen, _ = x.shape
    n = bsz * seqlen
    tm = 512
    h, main = _dense_pre(x.reshape(n, D_MODEL), p, tm)
    main3 = main.reshape(bsz, seqlen, W_MAIN_COLS)
    qt, ka, vt, k, v, logf, dtp = _qkv_post(main3, p["q_norm_e"], p["k_norm_e"], p["b_f"], 512)
    conv0 = jnp.zeros((bsz, D_CONV - 1, CONV_DIM), f32)
    h0 = jnp.zeros((bsz, SSD_HEADS * SSD_HEAD_DIM, D_STATE), f32)
    y_ssd, h_new, conv_new = _ssd(main3, dtp, conv0, h0, *_ssd_args(p), 2 * SSD_CHUNK, SSD_CHUNK)
    y_att = _flash_fox(qt, ka, vt.reshape(bsz, KV_HEADS, ATT_HEAD_DIM, seqlen), min(seqlen, 1024))
    y = _dense_post(y_ssd.reshape(n, D_SSM), y_att.reshape(n, ATT_DIM), main, h, p, tm)
    return (y.reshape(bsz, seqlen, D_MODEL),
            k.reshape(bsz, seqlen, KV_HEADS, ATT_HEAD_DIM), v.reshape(bsz, seqlen, KV_HEADS, ATT_HEAD_DIM),
            logf, h_new.reshape(bsz, SSD_HEADS, SSD_HEAD_DIM, D_STATE), conv_new)


def _sample_layer(x, conv_prev, h0, ck, cv, clf, page_table, p):
    bsz, t, _ = x.shape
    n = bsz * t
    tm = n
    h, main = _dense_pre(x.reshape(n, D_MODEL), p, tm)
    qt, _, _, k, v, logf, dtp = _qkv_post(main.reshape(1, n, W_MAIN_COLS), p["q_norm_e"], p["k_norm_e"], p["b_f"], n)
    main3 = main.reshape(bsz, t, W_MAIN_COLS)
    y_ssd, h_new, conv_new = _ssd(main3, dtp.reshape(bsz, t, 128), conv_prev,
                                  h0.reshape(bsz, SSD_HEADS * SSD_HEAD_DIM, D_STATE), *_ssd_args(p), t, 16)
    qg = qt[0, :, 0:ATT_HEAD_DIM, :].reshape(KV_HEADS, Q_PER_KV, ATT_HEAD_DIM, bsz, t).transpose(3, 0, 1, 4, 2)
    eye = jnp.eye(KV_HEADS, dtype=bf16)
    qbd = (qg[:, :, :, :, None, :] * eye[None, :, None, None, :, None]).reshape(bsz, ATT_HEADS * t, KV_DIM)
    page = ck.shape[1]
    padt = lambda a: jnp.pad(a.reshape(bsz, t, -1).transpose(0, 2, 1), ((0, 0), (0, 0), (0, page - t)))
    o = _paged_fox(page_table, qbd, padt(k), padt(v), padt(logf),
                   ck.transpose(0, 2, 3, 1), cv.transpose(0, 2, 3, 1), clf.transpose(0, 2, 1), t)
    o6 = o.reshape(bsz, KV_HEADS, Q_PER_KV, t, KV_HEADS, ATT_HEAD_DIM)
    y_att = jnp.stack([o6[:, g, :, :, g, :] for g in range(KV_HEADS)], axis=1)
    y_att = y_att.transpose(0, 3, 1, 2, 4).reshape(n, ATT_DIM).astype(bf16)
    y = _dense_post(y_ssd.reshape(n, D_SSM), y_att, main, h, p, tm)
    return (y.reshape(bsz, t, D_MODEL),
            k.reshape(bsz, t, KV_HEADS, ATT_HEAD_DIM), v.reshape(bsz, t, KV_HEADS, ATT_HEAD_DIM),
            logf.reshape(bsz, t, ATT_HEADS), h_new.reshape(bsz, SSD_HEADS, SSD_HEAD_DIM, D_STATE), conv_new)


def kernel(x_prompt, x_sample, cache_k, cache_v, cache_logf, state_ssm, state_conv, page_table, ffn1_norm, ffn1_w_gate, ffn1_w_up, ffn1_w_down, mix_norm, w_in, conv_w, conv_b, dt_bias, a_log, d_skip, ssd_norm, q_norm, k_norm, b_f, w_ssd_proj, w_attn_proj, w_out, ffn2_norm, ffn2_w_gate, ffn2_w_up, ffn2_w_down):
    depth = w_in.shape[0]
    yp, ys = x_prompt, x_sample
    prompt_states, sample_states = [], []
    for l in range(depth):
        p = {
            "ffn1_norm": ffn1_norm[l][None], "ffn1_wg": ffn1_w_gate[l].astype(bf16),
            "ffn1_wu": ffn1_w_up[l].astype(bf16), "ffn1_wd": ffn1_w_down[l].astype(bf16),
            "mix_norm": mix_norm[l][None], "w_main": _pack_w_in(w_in[l]),
            "conv_w": conv_w[l], "conv_b": conv_b[l][None], "dt_bias": dt_bias[l][None], "a_log": a_log[l][None],
            "d_skip_e": jnp.repeat(d_skip[l], SSD_HEAD_DIM)[None], "ssd_norm": ssd_norm[l][None],
            "q_norm_e": (jnp.tile(q_norm[l], ATT_HEADS) * (ATT_HEAD_DIM ** -0.5))[None], "k_norm_e": jnp.tile(k_norm[l], KV_HEADS)[None],
            "b_f": b_f[l][None],
            "w_ssd_proj": w_ssd_proj[l].astype(bf16), "w_attn_proj": w_attn_proj[l].astype(bf16),
            "w_out": w_out[l].astype(bf16),
            "ffn2_norm": ffn2_norm[l][None], "ffn2_wg": ffn2_w_gate[l].astype(bf16),
            "ffn2_wu": ffn2_w_up[l].astype(bf16), "ffn2_wd": ffn2_w_down[l].astype(bf16),
        }
        yp, *sp = _prompt_layer(yp, p)
        ys, *ss = _sample_layer(ys, state_conv[l], state_ssm[l], cache_k[l], cache_v[l], cache_logf[l],
                                page_table, p)
        prompt_states.append(sp)
        sample_states.append(ss)
    k_p, v_p, logf_p, ssm_p, conv_p = (jnp.stack(t) for t in zip(*prompt_states))
    k_s, v_s, logf_s, ssm_s, conv_s = (jnp.stack(t) for t in zip(*sample_states))
    return (yp, ys, k_p, v_p, logf_p, ssm_p, conv_p, k_s, v_s, logf_s, ssm_s, conv_s)
```

```python
import functools
import numpy as np
import jax
import jax.numpy as jnp
from jax import lax
from jax.experimental import pallas as pl
from jax.experimental.pallas import tpu as pltpu

f32 = jnp.float32
bf16 = jnp.bfloat16

RMS_EPS = 1e-6
D_MODEL = 1024
D_FF = 2816
D_SSM = 2048
SSD_HEAD_DIM = 64
SSD_HEADS = 32
SSD_GROUPS = 4
HEADS_PER_GROUP = SSD_HEADS // SSD_GROUPS
D_STATE = 128
D_CONV = 4
CONV_DIM = D_SSM + 2 * SSD_GROUPS * D_STATE
SSD_CHUNK = 128
ATT_HEAD_DIM = 64
ATT_HEADS = 16
KV_HEADS = 4
Q_PER_KV = ATT_HEADS // KV_HEADS
ATT_DIM = ATT_HEADS * ATT_HEAD_DIM
KV_DIM = KV_HEADS * ATT_HEAD_DIM
NEG_BIG = -1e30

COL_Z = 0
COL_GS = 2048
COL_GA = 3072
COL_Q = 4096
COL_KV = 5120
COL_XBC = 6144
W_MAIN_COLS = 9216

VMEM_LIMIT = 56 * 1024 * 1024


def _cparams(sem):
    return pltpu.CompilerParams(dimension_semantics=sem, vmem_limit_bytes=VMEM_LIMIT)


def _split2(x):
    hi = x.astype(bf16)
    lo = (x - hi.astype(f32)).astype(bf16)
    return hi, lo


def _split3(x):
    hi = x.astype(bf16)
    r = x - hi.astype(f32)
    mid = r.astype(bf16)
    lo = (r - mid.astype(f32)).astype(bf16)
    return hi, mid, lo


def _dot(a, b):
    return jnp.dot(a, b, preferred_element_type=f32)


def _dot_nt(a, b):
    return lax.dot_general(a, b, (((1,), (1,)), ((), ())), preferred_element_type=f32)


def _dot_tn(a, b):
    return lax.dot_general(a, b, (((0,), (0,)), ((), ())), preferred_element_type=f32)


def _dot2_lhs(x, m):
    hi, lo = _split2(x)
    return _dot(hi, m) + _dot(lo, m)


FFN_CHUNK = 256


def _rms(x, w):
    return x * lax.rsqrt(jnp.mean(x * x, axis=-1, keepdims=True) + RMS_EPS) * w


def _ffn_body(x_ref, nw_ref, wg_ref, wu_ref, wd_ref, nw2_ref, o_ref, *rest, emit_norm):
    a_ref = rest[-1]
    xn = _rms(x_ref[...], nw_ref[...]).astype(bf16)
    for c in range(D_FF // FFN_CHUNK):
        cols = slice(c * FFN_CHUNK, (c + 1) * FFN_CHUNK)
        g = _dot(xn, wg_ref[:, cols])
        u = _dot(xn, wu_ref[:, cols])
        a_ref[:, cols] = ((g * jax.nn.sigmoid(g)) * u).astype(bf16)
    y = x_ref[...] + 0.5 * _dot(a_ref[...], wd_ref[...])
    o_ref[...] = y
    if emit_norm:
        rest[0][...] = _rms(y, nw2_ref[...]).astype(bf16)


def _ffn(x, nw, wg, wu, wd, tm, nw2=None):
    n = x.shape[0]
    emit_norm = nw2 is not None
    full = lambda shp: pl.BlockSpec(shp, lambda i: (0, 0))
    row = pl.BlockSpec((tm, D_MODEL), lambda i: (i, 0))
    out_shape = [jax.ShapeDtypeStruct((n, D_MODEL), f32)]
    if emit_norm:
        out_shape.append(jax.ShapeDtypeStruct((n, D_MODEL), bf16))
    outs = pl.pallas_call(
        functools.partial(_ffn_body, emit_norm=emit_norm),
        grid=(n // tm,),
        in_specs=[row, full((1, D_MODEL)), full((D_MODEL, D_FF)), full((D_MODEL, D_FF)), full((D_FF, D_MODEL)),
                  full((1, D_MODEL))],
        out_specs=[row] * len(out_shape),
        out_shape=out_shape,
        scratch_shapes=[pltpu.VMEM((tm, D_FF), bf16)],
        compiler_params=_cparams(("parallel",)),
        name="ffn",
    )(x, nw, wg, wu, wd, nw2 if emit_norm else nw)
    return outs if emit_norm else outs[0]


def _proj_body(x_ref, wt_ref, o_ref):
    o_ref[...] = _dot_nt(x_ref[...], wt_ref[...])


def _in_proj(xn, wt, tm, tn):
    n = xn.shape[0]
    c = wt.shape[0]
    return pl.pallas_call(
        _proj_body,
        grid=(c // tn, n // tm),
        in_specs=[
            pl.BlockSpec((tm, D_MODEL), lambda j, i: (i, 0)),
            pl.BlockSpec((tn, D_MODEL), lambda j, i: (j, 0)),
        ],
        out_specs=pl.BlockSpec((tm, tn), lambda j, i: (i, j)),
        out_shape=jax.ShapeDtypeStruct((n, c), f32),
        compiler_params=_cparams(("parallel", "parallel")),
        name="in_proj",
    )(xn, wt)


AUG_DIM = 128
BIAS_PIECES = 3


def _qkv_body(q_ref, kv_ref, qw_ref, kw_ref, bf_ref, seg_ref, segt_ref, tril_ref, pk_ref, pc_ref,
              qt_ref, ka_ref, vt_ref, ko_ref, vo_ref, lf_ref, dtp_ref, carry_ref):
    @pl.when(pl.program_id(1) == 0)
    def _():
        carry_ref[...] = jnp.zeros_like(carry_ref)

    seg = seg_ref[...]
    segt = segt_ref[...]
    tl = q_ref.shape[0]

    q = q_ref[...]
    msq = _dot2_lhs(q * q, seg)
    fac = _dot2_lhs(lax.rsqrt(msq + RMS_EPS), segt)
    qnt = (q * fac * qw_ref[...]).T
    row = lax.broadcasted_iota(jnp.int32, (AUG_DIM - ATT_HEAD_DIM, tl), 0)
    for h in range(ATT_HEADS):
        qt_ref[h, 0:ATT_HEAD_DIM, :] = qnt[h * ATT_HEAD_DIM:(h + 1) * ATT_HEAD_DIM, :].astype(bf16)
        lo = BIAS_PIECES * (h % Q_PER_KV)
        qt_ref[h, ATT_HEAD_DIM:AUG_DIM, :] = jnp.where((row >= lo) & (row < lo + BIAS_PIECES), 1.0, 0.0).astype(bf16)

    kv = kv_ref[...]
    k = kv[:, 0:KV_DIM]
    msk = _dot2_lhs(k * k, seg[0:KV_DIM, :])
    fack = _dot2_lhs(lax.rsqrt(msk + RMS_EPS), segt[:, 0:KV_DIM])
    kn = k * fack * kw_ref[...]
    ko_ref[...] = kn
    v = kv[:, KV_DIM:2 * KV_DIM]
    vo_ref[...] = v
    vt_ref[...] = v.T.astype(bf16)
    dtp_ref[...] = kv[:, 2 * KV_DIM:2 * KV_DIM + 128]
    f_raw = kv[:, 2 * KV_DIM + SSD_HEADS:2 * KV_DIM + SSD_HEADS + ATT_HEADS]
    lf = jax.nn.log_sigmoid(f_raw + bf_ref[...])
    lf_ref[...] = lf

    tril = tril_ref[...]
    p0, p1, p2 = _split3(lf)
    c = _dot(tril, p0) + _dot(tril, p1) + _dot(tril, p2) + carry_ref[...]
    carry_ref[...] = c[tl - 1:tl, :]
    n0, n1, n2 = _split3(-c)
    ka = _dot(kn.astype(bf16), pk_ref[...]) + _dot(n0, pc_ref[0]) + _dot(n1, pc_ref[1]) + _dot(n2, pc_ref[2])
    for g in range(KV_HEADS):
        ka_ref[g] = ka[:, g * AUG_DIM:(g + 1) * AUG_DIM].astype(bf16)


def _qkv_post(main3, qw, kw, b_f, tl):
    bsz, seqlen, _ = main3.shape
    seg = np.zeros((ATT_DIM, 128), np.float32)
    for h in range(ATT_HEADS):
        seg[h * ATT_HEAD_DIM:(h + 1) * ATT_HEAD_DIM, h] = 1.0 / ATT_HEAD_DIM
    segt = (seg.T > 0).astype(np.float32)
    tril = np.tril(np.ones((tl, tl), np.float32))
    pk = np.zeros((KV_DIM, KV_HEADS * AUG_DIM), np.float32)
    pc = np.zeros((BIAS_PIECES, ATT_HEADS, KV_HEADS * AUG_DIM), np.float32)
    for g in range(KV_HEADS):
        for d in range(ATT_HEAD_DIM):
            pk[g * ATT_HEAD_DIM + d, g * AUG_DIM + d] = 1.0
        for j in range(Q_PER_KV):
            for i in range(BIAS_PIECES):
                pc[i, g * Q_PER_KV + j, g * AUG_DIM + ATT_HEAD_DIM + BIAS_PIECES * j + i] = 1.0
    full = lambda shp: pl.BlockSpec(shp, lambda b, i: tuple(0 for _ in shp))
    outs = pl.pallas_call(
        _qkv_body,
        grid=(bsz, seqlen // tl),
        in_specs=[
            pl.BlockSpec((None, tl, 1024), lambda b, i: (b, i, COL_Q // 1024)),
            pl.BlockSpec((None, tl, 1024), lambda b, i: (b, i, COL_KV // 1024)),
            full((1, ATT_DIM)), full((1, KV_DIM)), full((1, ATT_HEADS)),
            full((ATT_DIM, 128)), full((128, ATT_DIM)), full((tl, tl)),
            full((KV_DIM, KV_HEADS * AUG_DIM)), full((BIAS_PIECES, ATT_HEADS, KV_HEADS * AUG_DIM)),
        ],
        out_specs=[
            pl.BlockSpec((None, ATT_HEADS, AUG_DIM, tl), lambda b, i: (b, 0, 0, i)),
            pl.BlockSpec((None, KV_HEADS, tl, AUG_DIM), lambda b, i: (b, 0, i, 0)),
            pl.BlockSpec((None, KV_DIM, tl), lambda b, i: (b, 0, i)),
            pl.BlockSpec((None, tl, KV_DIM), lambda b, i: (b, i, 0)),
            pl.BlockSpec((None, tl, KV_DIM), lambda b, i: (b, i, 0)),
            pl.BlockSpec((None, tl, ATT_HEADS), lambda b, i: (b, i, 0)),
            pl.BlockSpec((None, tl, 128), lambda b, i: (b, i, 0)),
        ],
        out_shape=[
            jax.ShapeDtypeStruct((bsz, ATT_HEADS, AUG_DIM, seqlen), bf16),
            jax.ShapeDtypeStruct((bsz, KV_HEADS, seqlen, AUG_DIM), bf16),
            jax.ShapeDtypeStruct((bsz, KV_DIM, seqlen), bf16),
            jax.ShapeDtypeStruct((bsz, seqlen, KV_DIM), f32),
            jax.ShapeDtypeStruct((bsz, seqlen, KV_DIM), f32),
            jax.ShapeDtypeStruct((bsz, seqlen, ATT_HEADS), f32),
            jax.ShapeDtypeStruct((bsz, seqlen, 128), f32),
        ],
        scratch_shapes=[pltpu.VMEM((1, ATT_HEADS), f32)],
        compiler_params=_cparams(("parallel", "arbitrary")),
        name="qkv_post",
    )(main3, main3, qw, kw, b_f, jnp.asarray(seg, bf16), jnp.asarray(segt, bf16), jnp.asarray(tril, bf16),
      jnp.asarray(pk, bf16), jnp.asarray(pc, bf16))
    return outs


def _ssd_body(xbc_ref, z_ref, dtp_ref, prev_ref, h0_ref, cw_ref, cb_ref, dtb_ref, alog_ref, dskip_ref,
              nrm_ref, e2_ref, tril_ref, triu_ref, y_ref, hnew_ref, cnew_ref, xpad_ref, hs_ref, *, qr, q):
    c = pl.program_id(1)
    last = pl.num_programs(1) - 1

    @pl.when(c == 0)
    def _():
        xpad_ref[0:8, :] = jnp.zeros((8, CONV_DIM), f32)
        xpad_ref[8 - (D_CONV - 1):8, :] = prev_ref[...]
        hs_ref[...] = h0_ref[...]

    xpad_ref[8:8 + qr, :] = xbc_ref[...]
    nsub = max(1, qr // q)
    rows = min(q, qr)

    def conv_chunk(sub):
        r0 = sub * q
        xa = xpad_ref[r0:r0 + rows + 8, :]
        conv = cb_ref[...] + cw_ref[D_CONV - 1:D_CONV, :] * xa[8:8 + rows, :]
        for j in range(D_CONV - 1):
            shifted = pltpu.roll(xa, D_CONV - 1 - j, axis=0)[8:8 + rows, :]
            conv = conv + cw_ref[j:j + 1, :] * shifted
        xc = conv * jax.nn.sigmoid(conv)
        if q > rows:
            xc = jnp.concatenate([xc, jnp.zeros((q - rows, CONV_DIM), f32)], axis=0)
        return xc

    dt_all = jax.nn.softplus(dtp_ref[...][:, 0:SSD_HEADS] + dtb_ref[...])
    zz_all = z_ref[...]
    if q > rows:
        dt_all = jnp.concatenate([dt_all, jnp.zeros((q - rows, SSD_HEADS), f32)], axis=0)
        zz_all = jnp.concatenate([zz_all, jnp.zeros((q - rows, D_SSM), f32)], axis=0)
    xc_next = conv_chunk(0)
    for sub in range(nsub):
        xc = xc_next
        if sub + 1 < nsub:
            xc_next = conv_chunk(sub + 1)
        rows_in = slice(sub * q, (sub + 1) * q)
        y_ref[sub * q:sub * q + rows, :] = _ssd_chunk(
            xc, dt_all[rows_in], zz_all[rows_in], hs_ref, alog_ref, dskip_ref, nrm_ref, e2_ref,
            tril_ref, triu_ref, q)[0:rows, :].astype(bf16)

    @pl.when(c == last)
    def _():
        cnew_ref[...] = xpad_ref[qr + 5:qr + 8, :]
        hnew_ref[...] = hs_ref[...]

    xpad_ref[0:8, :] = xpad_ref[qr:qr + 8, :]


def _ssd_chunk(xc, dt, zz, hs_ref, alog_ref, dskip_ref, nrm_ref, e2_ref, tril_ref, triu_ref, q):
    xs = xc[:, 0:D_SSM]
    bm = xc[:, D_SSM:D_SSM + SSD_GROUPS * D_STATE].astype(bf16)
    cm = xc[:, D_SSM + SSD_GROUPS * D_STATE:CONV_DIM].astype(bf16)

    a = dt * (-jnp.exp(alog_ref[...]))
    tril = tril_ref[...]
    a0, a1, a2 = _split3(a)
    acs = _dot(tril, a0) + _dot(tril, a1) + _dot(tril, a2)
    triu = triu_ref[...]
    acst = _dot_tn(a0, triu) + _dot_tn(a1, triu) + _dot_tn(a2, triu)
    acs_last = acs[q - 1:q, :]
    fac = jnp.concatenate([dt, jnp.exp(acs_last - acs), jnp.exp(acs)], axis=0)
    fac_e = _dot(jnp.concatenate(_split2(fac), axis=1), e2_ref[...])
    xdt = xs * fac_e[0:q]
    xdtw = (xdt * fac_e[q:2 * q]).astype(bf16)
    eacs_e = fac_e[2 * q:3 * q]
    edec = jnp.exp(acst[:, q - 1:q])

    li = lax.broadcasted_iota(jnp.int32, (q, q), 0)
    si = lax.broadcasted_iota(jnp.int32, (q, q), 1)
    causal = li >= si
    lane = lax.broadcasted_iota(jnp.int32, (q, 2 * SSD_HEAD_DIM), 1)
    left = lane < SSD_HEAD_DIM

    y_groups = []
    for g in range(SSD_GROUPS):
        bg = bm[:, g * D_STATE:(g + 1) * D_STATE]
        cg = cm[:, g * D_STATE:(g + 1) * D_STATE]
        cbm = _dot_nt(cg, bg)
        r0 = g * HEADS_PER_GROUP * SSD_HEAD_DIM
        hg = hs_ref[r0:r0 + HEADS_PER_GROUP * SSD_HEAD_DIM, :].astype(bf16)
        y_off = _dot_nt(cg, hg)
        st = _dot_tn(xdtw[:, r0:r0 + HEADS_PER_GROUP * SSD_HEAD_DIM], bg)
        for hh in range(HEADS_PER_GROUP):
            h = g * HEADS_PER_GROUP + hh
            rows = slice(h * SSD_HEAD_DIM, (h + 1) * SSD_HEAD_DIM)
            hs_ref[rows, :] = (hs_ref[rows, :] * edec[h:h + 1, :]
                               + st[hh * SSD_HEAD_DIM:(hh + 1) * SSD_HEAD_DIM, :])
        pairs = []
        for pr in range(HEADS_PER_GROUP // 2):
            h0i = g * HEADS_PER_GROUP + 2 * pr
            xpair = xdt[:, h0i * SSD_HEAD_DIM:(h0i + 2) * SSD_HEAD_DIM]
            acc = None
            for t in range(2):
                h = h0i + t
                sg = acs[:, h:h + 1] - acst[h:h + 1, :]
                m = (cbm * jnp.exp(jnp.where(causal, sg, NEG_BIG))).astype(bf16)
                xm = jnp.where(left if t == 0 else jnp.logical_not(left), xpair, 0.0).astype(bf16)
                d = _dot(m, xm)
                acc = d if acc is None else acc + d
            pairs.append(acc)
        y_diag = jnp.concatenate(pairs, axis=1)
        y_groups.append(y_diag + y_off * eacs_e[:, r0:r0 + HEADS_PER_GROUP * SSD_HEAD_DIM])
    y = jnp.concatenate(y_groups, axis=1) + dskip_ref[...] * xs

    yg = y * (zz * jax.nn.sigmoid(zz))
    gw = D_SSM // SSD_GROUPS
    outs = []
    for g in range(SSD_GROUPS):
        t = yg[:, g * gw:(g + 1) * gw]
        outs.append(t * lax.rsqrt(jnp.mean(t * t, axis=-1, keepdims=True) + RMS_EPS))
    return jnp.concatenate(outs, axis=1) * nrm_ref[...]


def _ssd(main3, dtp, conv_prev, h0, conv_w, conv_b, dt_bias, a_log, d_skip_e, ssd_norm, qr, q):
    bsz, seqlen, _ = main3.shape
    nc = seqlen // qr
    e = np.zeros((SSD_HEADS, D_SSM), np.float32)
    for h in range(SSD_HEADS):
        e[h, h * SSD_HEAD_DIM:(h + 1) * SSD_HEAD_DIM] = 1.0
    tril = np.tril(np.ones((q, q), np.float32))
    nst = SSD_HEADS * SSD_HEAD_DIM
    full = lambda shp: pl.BlockSpec(shp, lambda b, c: tuple(0 for _ in shp))
    y, hnew, cnew = pl.pallas_call(
        functools.partial(_ssd_body, qr=qr, q=q),
        grid=(bsz, nc),
        in_specs=[
            pl.BlockSpec((None, qr, CONV_DIM), lambda b, c: (b, c, COL_XBC // CONV_DIM)),
            pl.BlockSpec((None, qr, D_SSM), lambda b, c: (b, c, COL_Z // D_SSM)),
            pl.BlockSpec((None, qr, 128), lambda b, c: (b, c, 0)),
            pl.BlockSpec((None, D_CONV - 1, CONV_DIM), lambda b, c: (b, 0, 0)),
            pl.BlockSpec((None, nst, D_STATE), lambda b, c: (b, 0, 0)),
            full((D_CONV, CONV_DIM)), full((1, CONV_DIM)), full((1, SSD_HEADS)), full((1, SSD_HEADS)),
            full((1, D_SSM)), full((1, D_SSM)), full((2 * SSD_HEADS, D_SSM)),
            full((q, q)), full((q, q)),
        ],
        out_specs=[
            pl.BlockSpec((None, qr, D_SSM), lambda b, c: (b, c, 0)),
            pl.BlockSpec((None, nst, D_STATE), lambda b, c: (b, 0, 0)),
            pl.BlockSpec((None, D_CONV - 1, CONV_DIM), lambda b, c: (b, 0, 0)),
        ],
        out_shape=[
            jax.ShapeDtypeStruct((bsz, seqlen, D_SSM), bf16),
            jax.ShapeDtypeStruct((bsz, nst, D_STATE), f32),
            jax.ShapeDtypeStruct((bsz, D_CONV - 1, CONV_DIM), f32),
        ],
        scratch_shapes=[pltpu.VMEM((qr + 8, CONV_DIM), f32), pltpu.VMEM((nst, D_STATE), f32)],
        compiler_params=_cparams(("parallel", "arbitrary")),
        name="ssd",
    )(main3, main3, dtp, conv_prev, h0, conv_w, conv_b, dt_bias, a_log, d_skip_e, ssd_norm,
      jnp.asarray(np.concatenate([e, e], axis=0), bf16), jnp.asarray(tril, bf16), jnp.asarray(tril.T, bf16))
    return y, hnew, cnew


FLASH_Q_LANES = 256
FLASH_KEY_ROWS = 128
FLASH_LOOKAHEAD = 6


def _flash_body(qi_ref, ki_ref, qt_ref, ka_ref, vt_ref, o_ref, m_ref, l_ref, acc_ref, *, tb):
    t = pl.program_id(2)
    qi = qi_ref[t]
    ki = ki_ref[t]
    ql, kr = FLASH_Q_LANES, FLASH_KEY_ROWS

    @pl.when(ki == 0)
    def _():
        m_ref[...] = jnp.full_like(m_ref, NEG_BIG)
        l_ref[...] = jnp.zeros_like(l_ref)
        acc_ref[...] = jnp.zeros_like(acc_ref)

    def process(diagonal):
        key_i = lax.broadcasted_iota(jnp.int32, (kr, ql), 0)
        qry_i = lax.broadcasted_iota(jnp.int32, (kr, ql), 1)
        tiles = [(j, r, kb) for kb in range(tb // kr) for j in range(Q_PER_KV) for r in range(tb // ql)
                 if not (diagonal and kb * kr > r * ql + ql - 1)]

        def scores(tile):
            j, r, kb = tile
            return _dot(ka_ref[kb * kr:(kb + 1) * kr, :], qt_ref[j, :, r * ql:(r + 1) * ql])

        def finish(tile, st):
            j, r, kb = tile
            cols = slice(r * ql, (r + 1) * ql)
            if diagonal and kb * kr + kr - 1 > r * ql:
                st = jnp.where(key_i + kb * kr <= qry_i + r * ql, st, NEG_BIG)
            st3 = st.reshape(kr // 8, 8, ql)
            mx = jnp.max(st3, axis=0)
            for sh in (4, 2, 1):
                mx = jnp.maximum(mx, pltpu.roll(mx, sh, axis=0))
            m_prev = m_ref[j, :, cols]
            m_new = jnp.maximum(m_prev, mx)
            alpha = jnp.exp(m_prev - m_new)
            p3 = jnp.exp(st3 - m_new[None])
            l_ref[j, :, cols] = alpha * l_ref[j, :, cols] + jnp.sum(p3, axis=0)
            pv = _dot(vt_ref[:, kb * kr:(kb + 1) * kr], p3.reshape(kr, ql).astype(bf16))
            acc3 = acc_ref[j, :, cols].reshape(ATT_HEAD_DIM // 8, 8, ql) * alpha[None]
            acc_ref[j, :, cols] = acc3.reshape(ATT_HEAD_DIM, ql) + pv
            m_ref[j, :, cols] = m_new

        pending = [scores(tl_) for tl_ in tiles[:FLASH_LOOKAHEAD]]
        for i, tile in enumerate(tiles):
            if i + FLASH_LOOKAHEAD < len(tiles):
                pending.append(scores(tiles[i + FLASH_LOOKAHEAD]))
            finish(tile, pending.pop(0))

    pl.when(ki < qi)(lambda: process(False))

    @pl.when(ki == qi)
    def _():
        process(True)
        o = acc_ref[...] / jnp.sum(l_ref[...], axis=1, keepdims=True)
        o_ref[...] = jnp.concatenate([o[j].T for j in range(Q_PER_KV)], axis=1).astype(bf16)


def _flash_fox(qt, ka, vt, tb):
    bsz, _, _, seqlen = qt.shape
    nb = seqlen // tb
    qi_tab = np.array([i for i in range(nb) for j in range(i + 1)], np.int32)
    ki_tab = np.array([j for i in range(nb) for j in range(i + 1)], np.int32)
    grid_spec = pltpu.PrefetchScalarGridSpec(
        num_scalar_prefetch=2,
        grid=(bsz, KV_HEADS, qi_tab.shape[0]),
        in_specs=[
            pl.BlockSpec((None, Q_PER_KV, AUG_DIM, tb), lambda b, g, t, qi, ki: (b, g, 0, qi[t])),
            pl.BlockSpec((None, None, tb, AUG_DIM), lambda b, g, t, qi, ki: (b, g, ki[t], 0)),
            pl.BlockSpec((None, None, ATT_HEAD_DIM, tb), lambda b, g, t, qi, ki: (b, g, 0, ki[t])),
        ],
        out_specs=pl.BlockSpec((None, tb, Q_PER_KV * ATT_HEAD_DIM), lambda b, g, t, qi, ki: (b, qi[t], g)),
        scratch_shapes=[
            pltpu.VMEM((Q_PER_KV, 8, tb), f32),
            pltpu.VMEM((Q_PER_KV, 8, tb), f32),
            pltpu.VMEM((Q_PER_KV, ATT_HEAD_DIM, tb), f32),
        ],
    )
    return pl.pallas_call(
        functools.partial(_flash_body, tb=tb),
        grid_spec=grid_spec,
        out_shape=jax.ShapeDtypeStruct((bsz, seqlen, ATT_DIM), bf16),
        compiler_params=_cparams(("parallel", "parallel", "arbitrary")),
        name="flash_fox",
    )(jnp.asarray(qi_tab), jnp.asarray(ki_tab), qt, ka, vt)


PAGES_PER_STEP = 32


def _lane_cumsum(x):
    n = x.shape[1]
    lane = lax.broadcasted_iota(jnp.int32, x.shape, 1)
    sh = 1
    while sh < n:
        x = x + jnp.where(lane >= sh, pltpu.roll(x, sh, axis=1), 0.0)
        sh *= 2
    return x


PAGE_SLOTS = 3


def _paged_body(pt_ref, qbd_ref, kn_ref, vn_ref, lfn_ref, ck_hbm, cv_hbm, clf_hbm, o_ref,
                kbuf, vbuf, lfbuf, sem, m_ref, l_ref, acc_ref, carry_ref, *, dec_seq):
    npg = PAGES_PER_STEP
    b = pl.program_id(0)
    j = pl.program_id(1)
    nj = pl.num_programs(1)
    step = b * nj + j
    n_steps = pl.num_programs(0) * nj
    rows = qbd_ref.shape[0]

    def page_copies(s):
        sb = s // nj
        sj = s - sb * nj
        slot = s % PAGE_SLOTS
        for i in range(npg):
            pid = pt_ref[sb, sj * npg + i]
            yield pltpu.make_async_copy(ck_hbm.at[pid], kbuf.at[slot, i], sem.at[0, slot])
            yield pltpu.make_async_copy(cv_hbm.at[pid], vbuf.at[slot, i], sem.at[1, slot])
            yield pltpu.make_async_copy(clf_hbm.at[pid], lfbuf.at[slot, i], sem.at[2, slot])

    @pl.when(step == 0)
    def _():
        for s in range(PAGE_SLOTS - 1):
            for cp in page_copies(s):
                cp.start()

    @pl.when(j == 0)
    def _():
        m_ref[...] = jnp.full_like(m_ref, NEG_BIG)
        l_ref[...] = jnp.zeros_like(l_ref)
        acc_ref[...] = jnp.zeros_like(acc_ref)
        carry_ref[...] = jnp.zeros_like(carry_ref)

    for cp in page_copies(step):
        cp.wait()
    slot = step % PAGE_SLOTS

    qbd = qbd_ref[...]

    def attend(kt, vt, lft, mask):
        nkeys = kt.shape[1]
        ct = _lane_cumsum(lft) + carry_ref[...]
        carry_ref[...] = ct[:, nkeys - 1:nkeys]
        s = _dot(qbd, kt.astype(bf16))
        s3 = s.reshape(ATT_HEADS, rows // ATT_HEADS, nkeys) - ct[:, None, :]
        s = s3.reshape(rows, nkeys)
        if mask is not None:
            s = jnp.where(mask, s, NEG_BIG)
        m_prev = m_ref[...]
        m_new = jnp.maximum(m_prev, jnp.max(s, axis=-1, keepdims=True))
        alpha = jnp.exp(m_prev - m_new)
        p = jnp.exp(s - m_new)
        l_ref[...] = alpha * l_ref[...] + jnp.sum(p, axis=-1, keepdims=True)
        acc_ref[...] = alpha * acc_ref[...] + _dot_nt(p.astype(bf16), vt.astype(bf16))
        m_ref[...] = m_new

    kt = jnp.concatenate([kbuf[slot, i].reshape(KV_DIM, -1) for i in range(npg)], axis=1)
    vt = jnp.concatenate([vbuf[slot, i].reshape(KV_DIM, -1) for i in range(npg)], axis=1)
    lft = jnp.concatenate([lfbuf[slot, i] for i in range(npg)], axis=1)
    attend(kt, vt, lft, None)

    @pl.when(j == nj - 1)
    def _():
        nk = kn_ref.shape[1]
        r = lax.broadcasted_iota(jnp.int32, (rows, nk), 0) % dec_seq
        t = lax.broadcasted_iota(jnp.int32, (rows, nk), 1)
        attend(kn_ref[...], vn_ref[...], lfn_ref[...], t <= r)
        o_ref[...] = acc_ref[...] / l_ref[...]

    @pl.when(step + (PAGE_SLOTS - 1) < n_steps)
    def _():
        for cp in page_copies(step + (PAGE_SLOTS - 1)):
            cp.start()


def _paged_fox(page_table, qbd, knt, vnt, lfnt, ckt, cvt, clft, dec_seq):
    bsz, n_pages = page_table.shape
    rows = qbd.shape[1]
    page = ckt.shape[3]
    npg = PAGES_PER_STEP
    assert n_pages % npg == 0 and bsz * (n_pages // npg) >= PAGE_SLOTS - 1
    bmap = lambda b, j, pt: (b, 0, 0)
    hbm = pl.BlockSpec(memory_space=pl.ANY)
    grid_spec = pltpu.PrefetchScalarGridSpec(
        num_scalar_prefetch=1,
        grid=(bsz, n_pages // npg),
        in_specs=[
            pl.BlockSpec((None, rows, KV_DIM), bmap),
            pl.BlockSpec((None, KV_DIM, knt.shape[2]), bmap),
            pl.BlockSpec((None, KV_DIM, vnt.shape[2]), bmap),
            pl.BlockSpec((None, ATT_HEADS, lfnt.shape[2]), bmap),
            hbm, hbm, hbm,
        ],
        out_specs=pl.BlockSpec((None, rows, KV_DIM), bmap),
        scratch_shapes=[
            pltpu.VMEM((PAGE_SLOTS, npg, KV_HEADS, ATT_HEAD_DIM, page), f32),
            pltpu.VMEM((PAGE_SLOTS, npg, KV_HEADS, ATT_HEAD_DIM, page), f32),
            pltpu.VMEM((PAGE_SLOTS, npg, ATT_HEADS, page), f32),
            pltpu.SemaphoreType.DMA((3, PAGE_SLOTS)),
            pltpu.VMEM((rows, 1), f32), pltpu.VMEM((rows, 1), f32), pltpu.VMEM((rows, KV_DIM), f32),
            pltpu.VMEM((ATT_HEADS, 1), f32),
        ],
    )
    return pl.pallas_call(
        functools.partial(_paged_body, dec_seq=dec_seq),
        grid_spec=grid_spec,
        out_shape=jax.ShapeDtypeStruct((bsz, rows, KV_DIM), f32),
        compiler_params=_cparams(("arbitrary", "arbitrary")),
        name="paged_fox",
    )(page_table, qbd, knt, vnt, lfnt, ckt, cvt, clft)


def _merge_body(ys_ref, ya_ref, gs_ref, ga_ref, h_ref, ws_ref, wa_ref, wo_ref, o_ref):
    a = jax.nn.sigmoid(gs_ref[...]) * _dot(ys_ref[...], ws_ref[...])
    b = jax.nn.sigmoid(ga_ref[...]) * _dot(ya_ref[...], wa_ref[...])
    o_ref[...] = h_ref[...] + _dot((a + b).astype(bf16), wo_ref[...])


def _merge(y_ssd, y_att, main, h, ws, wa, wo, tm):
    n = h.shape[0]
    full = lambda shp: pl.BlockSpec(shp, lambda i: (0, 0))
    return pl.pallas_call(
        _merge_body,
        grid=(n // tm,),
        in_specs=[
            pl.BlockSpec((tm, D_SSM), lambda i: (i, 0)),
            pl.BlockSpec((tm, ATT_DIM), lambda i: (i, 0)),
            pl.BlockSpec((tm, D_MODEL), lambda i: (i, COL_GS // D_MODEL)),
            pl.BlockSpec((tm, D_MODEL), lambda i: (i, COL_GA // D_MODEL)),
            pl.BlockSpec((tm, D_MODEL), lambda i: (i, 0)),
            full((D_SSM, D_MODEL)), full((ATT_DIM, D_MODEL)), full((D_MODEL, D_MODEL)),
        ],
        out_specs=pl.BlockSpec((tm, D_MODEL), lambda i: (i, 0)),
        out_shape=jax.ShapeDtypeStruct((n, D_MODEL), f32),
        compiler_params=_cparams(("parallel",)),
        name="merge",
    )(y_ssd, y_att, main, main, h, ws, wa, wo)


def _pack_w_in(w_in):
    o = np.cumsum([0, D_SSM, CONV_DIM, SSD_HEADS, ATT_DIM, KV_DIM, KV_DIM, ATT_HEADS, D_MODEL, D_MODEL])
    wt = w_in.T
    z, xbc, dt, q, k, v, f, gs, ga = (wt[int(o[i]):int(o[i + 1])] for i in range(9))
    pad = jnp.zeros((COL_XBC - COL_KV - 2 * KV_DIM - SSD_HEADS - ATT_HEADS, D_MODEL), w_in.dtype)
    return jnp.concatenate([z, gs, ga, q, k, v, dt, f, pad, xbc], axis=0).astype(bf16)


def _dense_pre(x2, p, tm):
    h, hn = _ffn(x2, p["ffn1_norm"], p["ffn1_wg"], p["ffn1_wu"], p["ffn1_wd"], tm, p["mix_norm"])
    main = _in_proj(hn, p["w_main"], min(x2.shape[0], 1024), W_MAIN_COLS // 4)
    return h, main


def _dense_post(y_ssd, y_att, main, h, p, tm):
    h2 = _merge(y_ssd, y_att, main, h, p["w_ssd_proj"], p["w_attn_proj"], p["w_out"], tm)
    return _ffn(h2, p["ffn2_norm"], p["ffn2_wg"], p["ffn2_wu"], p["ffn2_wd"], tm)


def _ssd_args(p):
    return (p["conv_w"], p["conv_b"], p["dt_bias"], p["a_log"], p["d_skip_e"], p["ssd_norm"])


def _prompt_layer(x, p):
    bsz, seqlen, _ = x.shape
    n = bsz * seqlen
    tm = 512
    h, main = _dense_pre(x.reshape(n, D_MODEL), p, tm)
    main3 = main.reshape(bsz, seqlen, W_MAIN_COLS)
    qt, ka, vt, k, v, logf, dtp = _qkv_post(main3, p["q_norm_e"], p["k_norm_e"], p["b_f"], 512)
    conv0 = jnp.zeros((bsz, D_CONV - 1, CONV_DIM), f32)
    h0 = jnp.zeros((bsz, SSD_HEADS * SSD_HEAD_DIM, D_STATE), f32)
    y_ssd, h_new, conv_new = _ssd(main3, dtp, conv0, h0, *_ssd_args(p), 4 * SSD_CHUNK, SSD_CHUNK)
    y_att = _flash_fox(qt, ka, vt.reshape(bsz, KV_HEADS, ATT_HEAD_DIM, seqlen), min(seqlen, 1024))
    y = _dense_post(y_ssd.reshape(n, D_SSM), y_att.reshape(n, ATT_DIM), main, h, p, tm)
    return (y.reshape(bsz, seqlen, D_MODEL),
            k.reshape(bsz, seqlen, KV_HEADS, ATT_HEAD_DIM), v.reshape(bsz, seqlen, KV_HEADS, ATT_HEAD_DIM),
            logf, h_new.reshape(bsz, SSD_HEADS, SSD_HEAD_DIM, D_STATE), conv_new)


def _sample_layer(x, conv_prev, h0, ck, cv, clf, page_table, p):
    bsz, t, _ = x.shape
    n = bsz * t
    tm = n
    h, main = _dense_pre(x.reshape(n, D_MODEL), p, tm)
    qt, _, _, k, v, logf, dtp = _qkv_post(main.reshape(1, n, W_MAIN_COLS), p["q_norm_e"], p["k_norm_e"], p["b_f"], n)
    main3 = main.reshape(bsz, t, W_MAIN_COLS)
    y_ssd, h_new, conv_new = _ssd(main3, dtp.reshape(bsz, t, 128), conv_prev,
                                  h0.reshape(bsz, SSD_HEADS * SSD_HEAD_DIM, D_STATE), *_ssd_args(p), t, 16)
    qg = qt[0, :, 0:ATT_HEAD_DIM, :].reshape(KV_HEADS, Q_PER_KV, ATT_HEAD_DIM, bsz, t).transpose(3, 0, 1, 4, 2)
    eye = jnp.eye(KV_HEADS, dtype=bf16)
    qbd = (qg[:, :, :, :, None, :] * eye[None, :, None, None, :, None]).reshape(bsz, ATT_HEADS * t, KV_DIM)
    page = ck.shape[1]
    padt = lambda a: jnp.pad(a.reshape(bsz, t, -1).transpose(0, 2, 1), ((0, 0), (0, 0), (0, page - t)))
    o = _paged_fox(page_table, qbd, padt(k), padt(v), padt(logf),
                   ck.transpose(0, 2, 3, 1), cv.transpose(0, 2, 3, 1), clf.transpose(0, 2, 1), t)
    o6 = o.reshape(bsz, KV_HEADS, Q_PER_KV, t, KV_HEADS, ATT_HEAD_DIM)
    y_att = jnp.stack([o6[:, g, :, :, g, :] for g in range(KV_HEADS)], axis=1)
    y_att = y_att.transpose(0, 3, 1, 2, 4).reshape(n, ATT_DIM).astype(bf16)
    y = _dense_post(y_ssd.reshape(n, D_SSM), y_att, main, h, p, tm)
    return (y.reshape(bsz, t, D_MODEL),
            k.reshape(bsz, t, KV_HEADS, ATT_HEAD_DIM), v.reshape(bsz, t, KV_HEADS, ATT_HEAD_DIM),
            logf.reshape(bsz, t, ATT_HEADS), h_new.reshape(bsz, SSD_HEADS, SSD_HEAD_DIM, D_STATE), conv_new)


def kernel(x_prompt, x_sample, cache_k, cache_v, cache_logf, state_ssm, state_conv, page_table, ffn1_norm, ffn1_w_gate, ffn1_w_up, ffn1_w_down, mix_norm, w_in, conv_w, conv_b, dt_bias, a_log, d_skip, ssd_norm, q_norm, k_norm, b_f, w_ssd_proj, w_attn_proj, w_out, ffn2_norm, ffn2_w_gate, ffn2_w_up, ffn2_w_down):
    depth = w_in.shape[0]
    yp, ys = x_prompt, x_sample
    prompt_states, sample_states = [], []
    for l in range(depth):
        p = {
            "ffn1_norm": ffn1_norm[l][None], "ffn1_wg": ffn1_w_gate[l].astype(bf16),
            "ffn1_wu": ffn1_w_up[l].astype(bf16), "ffn1_wd": ffn1_w_down[l].astype(bf16),
            "mix_norm": mix_norm[l][None], "w_main": _pack_w_in(w_in[l]),
            "conv_w": conv_w[l], "conv_b": conv_b[l][None], "dt_bias": dt_bias[l][None], "a_log": a_log[l][None],
            "d_skip_e": jnp.repeat(d_skip[l], SSD_HEAD_DIM)[None], "ssd_norm": ssd_norm[l][None],
            "q_norm_e": (jnp.tile(q_norm[l], ATT_HEADS) * (ATT_HEAD_DIM ** -0.5))[None], "k_norm_e": jnp.tile(k_norm[l], KV_HEADS)[None],
            "b_f": b_f[l][None],
            "w_ssd_proj": w_ssd_proj[l].astype(bf16), "w_attn_proj": w_attn_proj[l].astype(bf16),
            "w_out": w_out[l].astype(bf16),
            "ffn2_norm": ffn2_norm[l][None], "ffn2_wg": ffn2_w_gate[l].astype(bf16),
            "ffn2_wu": ffn2_w_up[l].astype(bf16), "ffn2_wd": ffn2_w_down[l].astype(bf16),
        }
        yp, *sp = _prompt_layer(yp, p)
        ys, *ss = _sample_layer(ys, state_conv[l], state_ssm[l], cache_k[l], cache_v[l], cache_logf[l],
                                page_table, p)
        prompt_states.append(sp)
        sample_states.append(ss)
    k_p, v_p, logf_p, ssm_p, conv_p = (jnp.stack(t) for t in zip(*prompt_states))
    k_s, v_s, logf_s, ssm_s, conv_s = (jnp.stack(t) for t in zip(*sample_states))
    return (yp, ys, k_p, v_p, logf_p, ssm_p, conv_p, k_s, v_s, logf_s, ssm_s, conv_s)
```

```python
import functools
import numpy as np
import jax
import jax.numpy as jnp
from jax import lax
from jax.experimental import pallas as pl
from jax.experimental.pallas import tpu as pltpu

f32 = jnp.float32
bf16 = jnp.bfloat16

RMS_EPS = 1e-6
D_MODEL = 1024
D_FF = 2816
D_SSM = 2048
SSD_HEAD_DIM = 64
SSD_HEADS = 32
SSD_GROUPS = 4
HEADS_PER_GROUP = SSD_HEADS // SSD_GROUPS
D_STATE = 128
D_CONV = 4
CONV_DIM = D_SSM + 2 * SSD_GROUPS * D_STATE
SSD_CHUNK = 128
ATT_HEAD_DIM = 64
ATT_HEADS = 16
KV_HEADS = 4
Q_PER_KV = ATT_HEADS // KV_HEADS
ATT_DIM = ATT_HEADS * ATT_HEAD_DIM
KV_DIM = KV_HEADS * ATT_HEAD_DIM
NEG_BIG = -1e30

COL_Z = 0
COL_GS = 2048
COL_GA = 3072
COL_Q = 4096
COL_KV = 5120
COL_XBC = 6144
W_MAIN_COLS = 9216

VMEM_LIMIT = 56 * 1024 * 1024


def _cparams(sem):
    return pltpu.CompilerParams(dimension_semantics=sem, vmem_limit_bytes=VMEM_LIMIT)


def _split2(x):
    hi = x.astype(bf16)
    lo = (x - hi.astype(f32)).astype(bf16)
    return hi, lo


def _split3(x):
    hi = x.astype(bf16)
    r = x - hi.astype(f32)
    mid = r.astype(bf16)
    lo = (r - mid.astype(f32)).astype(bf16)
    return hi, mid, lo


def _dot(a, b):
    return jnp.dot(a, b, preferred_element_type=f32)


def _dot_nt(a, b):
    return lax.dot_general(a, b, (((1,), (1,)), ((), ())), preferred_element_type=f32)


def _dot_tn(a, b):
    return lax.dot_general(a, b, (((0,), (0,)), ((), ())), preferred_element_type=f32)


def _dot2_lhs(x, m):
    hi, lo = _split2(x)
    return _dot(hi, m) + _dot(lo, m)


FFN_CHUNK = 256


def _rms(x, w):
    return x * lax.rsqrt(jnp.mean(x * x, axis=-1, keepdims=True) + RMS_EPS) * w


def _ffn_body(x_ref, nw_ref, wg_ref, wu_ref, wd_ref, nw2_ref, o_ref, *rest, emit_norm):
    a_ref = rest[-1]
    xn = _rms(x_ref[...], nw_ref[...]).astype(bf16)
    for c in range(D_FF // FFN_CHUNK):
        cols = slice(c * FFN_CHUNK, (c + 1) * FFN_CHUNK)
        g = _dot(xn, wg_ref[:, cols])
        u = _dot(xn, wu_ref[:, cols])
        a_ref[:, cols] = ((g * jax.nn.sigmoid(g)) * u).astype(bf16)
    y = x_ref[...] + 0.5 * _dot(a_ref[...], wd_ref[...])
    o_ref[...] = y
    if emit_norm:
        rest[0][...] = _rms(y, nw2_ref[...]).astype(bf16)


def _ffn(x, nw, wg, wu, wd, tm, nw2=None):
    n = x.shape[0]
    emit_norm = nw2 is not None
    full = lambda shp: pl.BlockSpec(shp, lambda i: (0, 0))
    row = pl.BlockSpec((tm, D_MODEL), lambda i: (i, 0))
    out_shape = [jax.ShapeDtypeStruct((n, D_MODEL), f32)]
    if emit_norm:
        out_shape.append(jax.ShapeDtypeStruct((n, D_MODEL), bf16))
    outs = pl.pallas_call(
        functools.partial(_ffn_body, emit_norm=emit_norm),
        grid=(n // tm,),
        in_specs=[row, full((1, D_MODEL)), full((D_MODEL, D_FF)), full((D_MODEL, D_FF)), full((D_FF, D_MODEL)),
                  full((1, D_MODEL))],
        out_specs=[row] * len(out_shape),
        out_shape=out_shape,
        scratch_shapes=[pltpu.VMEM((tm, D_FF), bf16)],
        compiler_params=_cparams(("parallel",)),
        name="ffn",
    )(x, nw, wg, wu, wd, nw2 if emit_norm else nw)
    return outs if emit_norm else outs[0]


def _proj_body(x_ref, wt_ref, o_ref):
    o_ref[...] = _dot_nt(x_ref[...], wt_ref[...])


def _in_proj(xn, wt, tm, tn):
    n = xn.shape[0]
    c = wt.shape[0]
    return pl.pallas_call(
        _proj_body,
        grid=(c // tn, n // tm),
        in_specs=[
            pl.BlockSpec((tm, D_MODEL), lambda j, i: (i, 0)),
            pl.BlockSpec((tn, D_MODEL), lambda j, i: (j, 0)),
        ],
        out_specs=pl.BlockSpec((tm, tn), lambda j, i: (i, j)),
        out_shape=jax.ShapeDtypeStruct((n, c), f32),
        compiler_params=_cparams(("parallel", "parallel")),
        name="in_proj",
    )(xn, wt)


AUG_DIM = 128
BIAS_PIECES = 3


def _qkv_body(q_ref, kv_ref, qw_ref, kw_ref, bf_ref, seg_ref, segt_ref, tril_ref, pk_ref, pc_ref,
              qt_ref, ka_ref, vt_ref, ko_ref, vo_ref, lf_ref, dtp_ref, carry_ref):
    @pl.when(pl.program_id(1) == 0)
    def _():
        carry_ref[...] = jnp.zeros_like(carry_ref)

    seg = seg_ref[...]
    segt = segt_ref[...]
    tl = q_ref.shape[0]

    q = q_ref[...]
    msq = _dot2_lhs(q * q, seg)
    fac = _dot2_lhs(lax.rsqrt(msq + RMS_EPS), segt)
    qnt = (q * fac * qw_ref[...]).T
    row = lax.broadcasted_iota(jnp.int32, (AUG_DIM - ATT_HEAD_DIM, tl), 0)
    for h in range(ATT_HEADS):
        qt_ref[h, 0:ATT_HEAD_DIM, :] = qnt[h * ATT_HEAD_DIM:(h + 1) * ATT_HEAD_DIM, :].astype(bf16)
        lo = BIAS_PIECES * (h % Q_PER_KV)
        qt_ref[h, ATT_HEAD_DIM:AUG_DIM, :] = jnp.where((row >= lo) & (row < lo + BIAS_PIECES), 1.0, 0.0).astype(bf16)

    kv = kv_ref[...]
    k = kv[:, 0:KV_DIM]
    msk = _dot2_lhs(k * k, seg[0:KV_DIM, :])
    fack = _dot2_lhs(lax.rsqrt(msk + RMS_EPS), segt[:, 0:KV_DIM])
    kn = k * fack * kw_ref[...]
    ko_ref[...] = kn
    v = kv[:, KV_DIM:2 * KV_DIM]
    vo_ref[...] = v
    vt_ref[...] = v.T.astype(bf16)
    dtp_ref[...] = kv[:, 2 * KV_DIM:2 * KV_DIM + 128]
    f_raw = kv[:, 2 * KV_DIM + SSD_HEADS:2 * KV_DIM + SSD_HEADS + ATT_HEADS]
    lf = jax.nn.log_sigmoid(f_raw + bf_ref[...])
    lf_ref[...] = lf

    tril = tril_ref[...]
    p0, p1, p2 = _split3(lf)
    c = _dot(tril, p0) + _dot(tril, p1) + _dot(tril, p2) + carry_ref[...]
    carry_ref[...] = c[tl - 1:tl, :]
    n0, n1, n2 = _split3(-c)
    ka = _dot(kn.astype(bf16), pk_ref[...]) + _dot(n0, pc_ref[0]) + _dot(n1, pc_ref[1]) + _dot(n2, pc_ref[2])
    for g in range(KV_HEADS):
        ka_ref[g] = ka[:, g * AUG_DIM:(g + 1) * AUG_DIM].astype(bf16)


def _qkv_post(main3, qw, kw, b_f, tl):
    bsz, seqlen, _ = main3.shape
    seg = np.zeros((ATT_DIM, 128), np.float32)
    for h in range(ATT_HEADS):
        seg[h * ATT_HEAD_DIM:(h + 1) * ATT_HEAD_DIM, h] = 1.0 / ATT_HEAD_DIM
    segt = (seg.T > 0).astype(np.float32)
    tril = np.tril(np.ones((tl, tl), np.float32))
    pk = np.zeros((KV_DIM, KV_HEADS * AUG_DIM), np.float32)
    pc = np.zeros((BIAS_PIECES, ATT_HEADS, KV_HEADS * AUG_DIM), np.float32)
    for g in range(KV_HEADS):
        for d in range(ATT_HEAD_DIM):
            pk[g * ATT_HEAD_DIM + d, g * AUG_DIM + d] = 1.0
        for j in range(Q_PER_KV):
            for i in range(BIAS_PIECES):
                pc[i, g * Q_PER_KV + j, g * AUG_DIM + ATT_HEAD_DIM + BIAS_PIECES * j + i] = 1.0
    full = lambda shp: pl.BlockSpec(shp, lambda b, i: tuple(0 for _ in shp))
    outs = pl.pallas_call(
        _qkv_body,
        grid=(bsz, seqlen // tl),
        in_specs=[
            pl.BlockSpec((None, tl, 1024), lambda b, i: (b, i, COL_Q // 1024)),
            pl.BlockSpec((None, tl, 1024), lambda b, i: (b, i, COL_KV // 1024)),
            full((1, ATT_DIM)), full((1, KV_DIM)), full((1, ATT_HEADS)),
            full((ATT_DIM, 128)), full((128, ATT_DIM)), full((tl, tl)),
            full((KV_DIM, KV_HEADS * AUG_DIM)), full((BIAS_PIECES, ATT_HEADS, KV_HEADS * AUG_DIM)),
        ],
        out_specs=[
            pl.BlockSpec((None, ATT_HEADS, AUG_DIM, tl), lambda b, i: (b, 0, 0, i)),
            pl.BlockSpec((None, KV_HEADS, tl, AUG_DIM), lambda b, i: (b, 0, i, 0)),
            pl.BlockSpec((None, KV_DIM, tl), lambda b, i: (b, 0, i)),
            pl.BlockSpec((None, tl, KV_DIM), lambda b, i: (b, i, 0)),
            pl.BlockSpec((None, tl, KV_DIM), lambda b, i: (b, i, 0)),
            pl.BlockSpec((None, tl, ATT_HEADS), lambda b, i: (b, i, 0)),
            pl.BlockSpec((None, tl, 128), lambda b, i: (b, i, 0)),
        ],
        out_shape=[
            jax.ShapeDtypeStruct((bsz, ATT_HEADS, AUG_DIM, seqlen), bf16),
            jax.ShapeDtypeStruct((bsz, KV_HEADS, seqlen, AUG_DIM), bf16),
            jax.ShapeDtypeStruct((bsz, KV_DIM, seqlen), bf16),
            jax.ShapeDtypeStruct((bsz, seqlen, KV_DIM), f32),
            jax.ShapeDtypeStruct((bsz, seqlen, KV_DIM), f32),
            jax.ShapeDtypeStruct((bsz, seqlen, ATT_HEADS), f32),
            jax.ShapeDtypeStruct((bsz, seqlen, 128), f32),
        ],
        scratch_shapes=[pltpu.VMEM((1, ATT_HEADS), f32)],
        compiler_params=_cparams(("parallel", "arbitrary")),
        name="qkv_post",
    )(main3, main3, qw, kw, b_f, jnp.asarray(seg, bf16), jnp.asarray(segt, bf16), jnp.asarray(tril, bf16),
      jnp.asarray(pk, bf16), jnp.asarray(pc, bf16))
    return outs


def _ssd_body(xbc_ref, z_ref, dtp_ref, prev_ref, h0_ref, cw_ref, cb_ref, dtb_ref, alog_ref, dskip_ref,
              nrm_ref, e2_ref, tril_ref, triu_ref, y_ref, hnew_ref, cnew_ref, xpad_ref, hs_ref, *, qr, q):
    c = pl.program_id(1)
    last = pl.num_programs(1) - 1

    @pl.when(c == 0)
    def _():
        xpad_ref[0:8, :] = jnp.zeros((8, CONV_DIM), f32)
        xpad_ref[8 - (D_CONV - 1):8, :] = prev_ref[...]
        hs_ref[...] = h0_ref[...]

    xpad_ref[8:8 + qr, :] = xbc_ref[...]
    nsub = max(1, qr // q)
    rows = min(q, qr)

    def conv_chunk(sub):
        r0 = sub * q
        xa = xpad_ref[r0:r0 + rows + 8, :]
        conv = cb_ref[...] + cw_ref[D_CONV - 1:D_CONV, :] * xa[8:8 + rows, :]
        for j in range(D_CONV - 1):
            shifted = pltpu.roll(xa, D_CONV - 1 - j, axis=0)[8:8 + rows, :]
            conv = conv + cw_ref[j:j + 1, :] * shifted
        xc = conv * jax.nn.sigmoid(conv)
        if q > rows:
            xc = jnp.concatenate([xc, jnp.zeros((q - rows, CONV_DIM), f32)], axis=0)
        return xc

    dt_all = jax.nn.softplus(dtp_ref[...][:, 0:SSD_HEADS] + dtb_ref[...])
    zz_all = z_ref[...]
    if q > rows:
        dt_all = jnp.concatenate([dt_all, jnp.zeros((q - rows, SSD_HEADS), f32)], axis=0)
        zz_all = jnp.concatenate([zz_all, jnp.zeros((q - rows, D_SSM), f32)], axis=0)
    xc_next = conv_chunk(0)
    for sub in range(nsub):
        xc = xc_next
        if sub + 1 < nsub:
            xc_next = conv_chunk(sub + 1)
        rows_in = slice(sub * q, (sub + 1) * q)
        y_ref[sub * q:sub * q + rows, :] = _ssd_chunk(
            xc, dt_all[rows_in], zz_all[rows_in], hs_ref, alog_ref, dskip_ref, nrm_ref, e2_ref,
            tril_ref, triu_ref, q)[0:rows, :].astype(bf16)

    @pl.when(c == last)
    def _():
        cnew_ref[...] = xpad_ref[qr + 5:qr + 8, :]
        hnew_ref[...] = hs_ref[...]

    xpad_ref[0:8, :] = xpad_ref[qr:qr + 8, :]


def _ssd_chunk(xc, dt, zz, hs_ref, alog_ref, dskip_ref, nrm_ref, e2_ref, tril_ref, triu_ref, q):
    xs = xc[:, 0:D_SSM]
    bm = xc[:, D_SSM:D_SSM + SSD_GROUPS * D_STATE].astype(bf16)
    cm = xc[:, D_SSM + SSD_GROUPS * D_STATE:CONV_DIM].astype(bf16)

    a = dt * (-jnp.exp(alog_ref[...]))
    tril = tril_ref[...]
    a0, a1, a2 = _split3(a)
    acs = _dot(tril, a0) + _dot(tril, a1) + _dot(tril, a2)
    triu = triu_ref[...]
    acst = _dot_tn(a0, triu) + _dot_tn(a1, triu) + _dot_tn(a2, triu)
    acs_last = acs[q - 1:q, :]
    fac = jnp.concatenate([dt, jnp.exp(acs_last - acs), jnp.exp(acs)], axis=0)
    fac_e = _dot(jnp.concatenate(_split2(fac), axis=1), e2_ref[...])
    xdt = xs * fac_e[0:q]
    xdtw = (xdt * fac_e[q:2 * q]).astype(bf16)
    eacs_e = fac_e[2 * q:3 * q]
    edec = jnp.exp(acst[:, q - 1:q])

    li = lax.broadcasted_iota(jnp.int32, (q, q), 0)
    si = lax.broadcasted_iota(jnp.int32, (q, q), 1)
    causal = li >= si
    lane = lax.broadcasted_iota(jnp.int32, (q, 2 * SSD_HEAD_DIM), 1)
    left = lane < SSD_HEAD_DIM

    y_groups = []
    for g in range(SSD_GROUPS):
        bg = bm[:, g * D_STATE:(g + 1) * D_STATE]
        cg = cm[:, g * D_STATE:(g + 1) * D_STATE]
        cbm = _dot_nt(cg, bg)
        r0 = g * HEADS_PER_GROUP * SSD_HEAD_DIM
        hg = hs_ref[r0:r0 + HEADS_PER_GROUP * SSD_HEAD_DIM, :].astype(bf16)
        y_off = _dot_nt(cg, hg)
        st = _dot_tn(xdtw[:, r0:r0 + HEADS_PER_GROUP * SSD_HEAD_DIM], bg)
        for hh in range(HEADS_PER_GROUP):
            h = g * HEADS_PER_GROUP + hh
            rows = slice(h * SSD_HEAD_DIM, (h + 1) * SSD_HEAD_DIM)
            hs_ref[rows, :] = (hs_ref[rows, :] * edec[h:h + 1, :]
                               + st[hh * SSD_HEAD_DIM:(hh + 1) * SSD_HEAD_DIM, :])
        pairs = []
        for pr in range(HEADS_PER_GROUP // 2):
            h0i = g * HEADS_PER_GROUP + 2 * pr
            xpair = xdt[:, h0i * SSD_HEAD_DIM:(h0i + 2) * SSD_HEAD_DIM]
            acc = None
            for t in range(2):
                h = h0i + t
                sg = acs[:, h:h + 1] - acst[h:h + 1, :]
                m = (cbm * jnp.exp(jnp.where(causal, sg, NEG_BIG))).astype(bf16)
                xm = jnp.where(left if t == 0 else jnp.logical_not(left), xpair, 0.0).astype(bf16)
                d = _dot(m, xm)
                acc = d if acc is None else acc + d
            pairs.append(acc)
        y_diag = jnp.concatenate(pairs, axis=1)
        y_groups.append(y_diag + y_off * eacs_e[:, r0:r0 + HEADS_PER_GROUP * SSD_HEAD_DIM])
    y = jnp.concatenate(y_groups, axis=1) + dskip_ref[...] * xs

    yg = y * (zz * jax.nn.sigmoid(zz))
    gw = D_SSM // SSD_GROUPS
    outs = []
    for g in range(SSD_GROUPS):
        t = yg[:, g * gw:(g + 1) * gw]
        outs.append(t * lax.rsqrt(jnp.mean(t * t, axis=-1, keepdims=True) + RMS_EPS))
    return jnp.concatenate(outs, axis=1) * nrm_ref[...]


def _ssd(main3, dtp, conv_prev, h0, conv_w, conv_b, dt_bias, a_log, d_skip_e, ssd_norm, qr, q):
    bsz, seqlen, _ = main3.shape
    nc = seqlen // qr
    e = np.zeros((SSD_HEADS, D_SSM), np.float32)
    for h in range(SSD_HEADS):
        e[h, h * SSD_HEAD_DIM:(h + 1) * SSD_HEAD_DIM] = 1.0
    tril = np.tril(np.ones((q, q), np.float32))
    nst = SSD_HEADS * SSD_HEAD_DIM
    full = lambda shp: pl.BlockSpec(shp, lambda b, c: tuple(0 for _ in shp))
    y, hnew, cnew = pl.pallas_call(
        functools.partial(_ssd_body, qr=qr, q=q),
        grid=(bsz, nc),
        in_specs=[
            pl.BlockSpec((None, qr, CONV_DIM), lambda b, c: (b, c, COL_XBC // CONV_DIM)),
            pl.BlockSpec((None, qr, D_SSM), lambda b, c: (b, c, COL_Z // D_SSM)),
            pl.BlockSpec((None, qr, 128), lambda b, c: (b, c, 0)),
            pl.BlockSpec((None, D_CONV - 1, CONV_DIM), lambda b, c: (b, 0, 0)),
            pl.BlockSpec((None, nst, D_STATE), lambda b, c: (b, 0, 0)),
            full((D_CONV, CONV_DIM)), full((1, CONV_DIM)), full((1, SSD_HEADS)), full((1, SSD_HEADS)),
            full((1, D_SSM)), full((1, D_SSM)), full((2 * SSD_HEADS, D_SSM)),
            full((q, q)), full((q, q)),
        ],
        out_specs=[
            pl.BlockSpec((None, qr, D_SSM), lambda b, c: (b, c, 0)),
            pl.BlockSpec((None, nst, D_STATE), lambda b, c: (b, 0, 0)),
            pl.BlockSpec((None, D_CONV - 1, CONV_DIM), lambda b, c: (b, 0, 0)),
        ],
        out_shape=[
            jax.ShapeDtypeStruct((bsz, seqlen, D_SSM), bf16),
            jax.ShapeDtypeStruct((bsz, nst, D_STATE), f32),
            jax.ShapeDtypeStruct((bsz, D_CONV - 1, CONV_DIM), f32),
        ],
        scratch_shapes=[pltpu.VMEM((qr + 8, CONV_DIM), f32), pltpu.VMEM((nst, D_STATE), f32)],
        compiler_params=_cparams(("parallel", "arbitrary")),
        name="ssd",
    )(main3, main3, dtp, conv_prev, h0, conv_w, conv_b, dt_bias, a_log, d_skip_e, ssd_norm,
      jnp.asarray(np.concatenate([e, e], axis=0), bf16), jnp.asarray(tril, bf16), jnp.asarray(tril.T, bf16))
    return y, hnew, cnew


FLASH_Q_LANES = 256
FLASH_KEY_ROWS = 128
FLASH_LOOKAHEAD = 6


def _flash_body(qi_ref, ki_ref, qt_ref, ka_ref, vt_ref, o_ref, m_ref, l_ref, acc_ref, *, tb):
    t = pl.program_id(2)
    qi = qi_ref[t]
    ki = ki_ref[t]
    ql, kr = FLASH_Q_LANES, FLASH_KEY_ROWS

    @pl.when(ki == 0)
    def _():
        m_ref[...] = jnp.full_like(m_ref, NEG_BIG)
        l_ref[...] = jnp.zeros_like(l_ref)
        acc_ref[...] = jnp.zeros_like(acc_ref)

    def process(diagonal):
        key_i = lax.broadcasted_iota(jnp.int32, (kr, ql), 0)
        qry_i = lax.broadcasted_iota(jnp.int32, (kr, ql), 1)
        tiles = [(j, r, kb) for kb in range(tb // kr) for j in range(Q_PER_KV) for r in range(tb // ql)
                 if not (diagonal and kb * kr > r * ql + ql - 1)]

        def scores(tile):
            j, r, kb = tile
            return _dot(ka_ref[kb * kr:(kb + 1) * kr, :], qt_ref[j, :, r * ql:(r + 1) * ql])

        def finish(tile, st):
            j, r, kb = tile
            cols = slice(r * ql, (r + 1) * ql)
            if diagonal and kb * kr + kr - 1 > r * ql:
                st = jnp.where(key_i + kb * kr <= qry_i + r * ql, st, NEG_BIG)
            st3 = st.reshape(kr // 8, 8, ql)
            mx = jnp.max(st3, axis=0)
            for sh in (4, 2, 1):
                mx = jnp.maximum(mx, pltpu.roll(mx, sh, axis=0))
            m_prev = m_ref[j, :, cols]
            m_new = jnp.maximum(m_prev, mx)
            alpha = jnp.exp(m_prev - m_new)
            p3 = jnp.exp(st3 - m_new[None])
            l_ref[j, :, cols] = alpha * l_ref[j, :, cols] + jnp.sum(p3, axis=0)
            pv = _dot(vt_ref[:, kb * kr:(kb + 1) * kr], p3.reshape(kr, ql).astype(bf16))
            acc3 = acc_ref[j, :, cols].reshape(ATT_HEAD_DIM // 8, 8, ql) * alpha[None]
            acc_ref[j, :, cols] = acc3.reshape(ATT_HEAD_DIM, ql) + pv
            m_ref[j, :, cols] = m_new

        pending = [scores(tl_) for tl_ in tiles[:FLASH_LOOKAHEAD]]
        for i, tile in enumerate(tiles):
            if i + FLASH_LOOKAHEAD < len(tiles):
                pending.append(scores(tiles[i + FLASH_LOOKAHEAD]))
            finish(tile, pending.pop(0))

    pl.when(ki < qi)(lambda: process(False))

    @pl.when(ki == qi)
    def _():
        process(True)
        o = acc_ref[...] / jnp.sum(l_ref[...], axis=1, keepdims=True)
        o_ref[...] = jnp.concatenate([o[j].T for j in range(Q_PER_KV)], axis=1).astype(bf16)


def _flash_fox(qt, ka, vt, tb):
    bsz, _, _, seqlen = qt.shape
    nb = seqlen // tb
    qi_tab = np.array([i for i in range(nb) for j in range(i + 1)], np.int32)
    ki_tab = np.array([j for i in range(nb) for j in range(i + 1)], np.int32)
    grid_spec = pltpu.PrefetchScalarGridSpec(
        num_scalar_prefetch=2,
        grid=(bsz, KV_HEADS, qi_tab.shape[0]),
        in_specs=[
            pl.BlockSpec((None, Q_PER_KV, AUG_DIM, tb), lambda b, g, t, qi, ki: (b, g, 0, qi[t])),
            pl.BlockSpec((None, None, tb, AUG_DIM), lambda b, g, t, qi, ki: (b, g, ki[t], 0)),
            pl.BlockSpec((None, None, ATT_HEAD_DIM, tb), lambda b, g, t, qi, ki: (b, g, 0, ki[t])),
        ],
        out_specs=pl.BlockSpec((None, tb, Q_PER_KV * ATT_HEAD_DIM), lambda b, g, t, qi, ki: (b, qi[t], g)),
        scratch_shapes=[
            pltpu.VMEM((Q_PER_KV, 8, tb), f32),
            pltpu.VMEM((Q_PER_KV, 8, tb), f32),
            pltpu.VMEM((Q_PER_KV, ATT_HEAD_DIM, tb), f32),
        ],
    )
    return pl.pallas_call(
        functools.partial(_flash_body, tb=tb),
        grid_spec=grid_spec,
        out_shape=jax.ShapeDtypeStruct((bsz, seqlen, ATT_DIM), bf16),
        compiler_params=_cparams(("parallel", "parallel", "arbitrary")),
        name="flash_fox",
    )(jnp.asarray(qi_tab), jnp.asarray(ki_tab), qt, ka, vt)


PAGES_PER_STEP = 32


def _lane_cumsum(x):
    n = x.shape[1]
    lane = lax.broadcasted_iota(jnp.int32, x.shape, 1)
    sh = 1
    while sh < n:
        x = x + jnp.where(lane >= sh, pltpu.roll(x, sh, axis=1), 0.0)
        sh *= 2
    return x


PAGE_SLOTS = 3


def _paged_body(pt_ref, qbd_ref, kn_ref, vn_ref, lfn_ref, ck_hbm, cv_hbm, clf_hbm, o_ref,
                kbuf, vbuf, lfbuf, sem, m_ref, l_ref, acc_ref, carry_ref, *, dec_seq):
    npg = PAGES_PER_STEP
    b = pl.program_id(0)
    j = pl.program_id(1)
    nj = pl.num_programs(1)
    step = b * nj + j
    n_steps = pl.num_programs(0) * nj
    rows = qbd_ref.shape[0]

    def page_copies(s):
        sb = s // nj
        sj = s - sb * nj
        slot = s % PAGE_SLOTS
        for i in range(npg):
            pid = pt_ref[sb, sj * npg + i]
            yield pltpu.make_async_copy(ck_hbm.at[pid], kbuf.at[slot, i], sem.at[0, slot])
            yield pltpu.make_async_copy(cv_hbm.at[pid], vbuf.at[slot, i], sem.at[1, slot])
            yield pltpu.make_async_copy(clf_hbm.at[pid], lfbuf.at[slot, i], sem.at[2, slot])

    @pl.when(step == 0)
    def _():
        for s in range(PAGE_SLOTS - 1):
            for cp in page_copies(s):
                cp.start()

    @pl.when(j == 0)
    def _():
        m_ref[...] = jnp.full_like(m_ref, NEG_BIG)
        l_ref[...] = jnp.zeros_like(l_ref)
        acc_ref[...] = jnp.zeros_like(acc_ref)
        carry_ref[...] = jnp.zeros_like(carry_ref)

    for cp in page_copies(step):
        cp.wait()
    slot = step % PAGE_SLOTS

    qbd = qbd_ref[...]

    def attend(kt, vt, lft, mask):
        nkeys = kt.shape[1]
        ct = _lane_cumsum(lft) + carry_ref[...]
        carry_ref[...] = ct[:, nkeys - 1:nkeys]
        s = _dot(qbd, kt.astype(bf16))
        s3 = s.reshape(ATT_HEADS, rows // ATT_HEADS, nkeys) - ct[:, None, :]
        s = s3.reshape(rows, nkeys)
        if mask is not None:
            s = jnp.where(mask, s, NEG_BIG)
        m_prev = m_ref[...]
        m_new = jnp.maximum(m_prev, jnp.max(s, axis=-1, keepdims=True))
        alpha = jnp.exp(m_prev - m_new)
        p = jnp.exp(s - m_new)
        l_ref[...] = alpha * l_ref[...] + jnp.sum(p, axis=-1, keepdims=True)
        acc_ref[...] = alpha * acc_ref[...] + _dot_nt(p.astype(bf16), vt.astype(bf16))
        m_ref[...] = m_new

    kt = jnp.concatenate([kbuf[slot, i].reshape(KV_DIM, -1) for i in range(npg)], axis=1)
    vt = jnp.concatenate([vbuf[slot, i].reshape(KV_DIM, -1) for i in range(npg)], axis=1)
    lft = jnp.concatenate([lfbuf[slot, i] for i in range(npg)], axis=1)
    attend(kt, vt, lft, None)

    @pl.when(j == nj - 1)
    def _():
        nk = kn_ref.shape[1]
        r = lax.broadcasted_iota(jnp.int32, (rows, nk), 0) % dec_seq
        t = lax.broadcasted_iota(jnp.int32, (rows, nk), 1)
        attend(kn_ref[...], vn_ref[...], lfn_ref[...], t <= r)
        o_ref[...] = acc_ref[...] / l_ref[...]

    @pl.when(step + (PAGE_SLOTS - 1) < n_steps)
    def _():
        for cp in page_copies(step + (PAGE_SLOTS - 1)):
            cp.start()


def _paged_fox(page_table, qbd, knt, vnt, lfnt, ckt, cvt, clft, dec_seq):
    bsz, n_pages = page_table.shape
    rows = qbd.shape[1]
    page = ckt.shape[3]
    npg = PAGES_PER_STEP
    assert n_pages % npg == 0 and bsz * (n_pages // npg) >= PAGE_SLOTS - 1
    bmap = lambda b, j, pt: (b, 0, 0)
    hbm = pl.BlockSpec(memory_space=pl.ANY)
    grid_spec = pltpu.PrefetchScalarGridSpec(
        num_scalar_prefetch=1,
        grid=(bsz, n_pages // npg),
        in_specs=[
            pl.BlockSpec((None, rows, KV_DIM), bmap),
            pl.BlockSpec((None, KV_DIM, knt.shape[2]), bmap),
            pl.BlockSpec((None, KV_DIM, vnt.shape[2]), bmap),
            pl.BlockSpec((None, ATT_HEADS, lfnt.shape[2]), bmap),
            hbm, hbm, hbm,
        ],
        out_specs=pl.BlockSpec((None, rows, KV_DIM), bmap),
        scratch_shapes=[
            pltpu.VMEM((PAGE_SLOTS, npg, KV_HEADS, ATT_HEAD_DIM, page), f32),
            pltpu.VMEM((PAGE_SLOTS, npg, KV_HEADS, ATT_HEAD_DIM, page), f32),
            pltpu.VMEM((PAGE_SLOTS, npg, ATT_HEADS, page), f32),
            pltpu.SemaphoreType.DMA((3, PAGE_SLOTS)),
            pltpu.VMEM((rows, 1), f32), pltpu.VMEM((rows, 1), f32), pltpu.VMEM((rows, KV_DIM), f32),
            pltpu.VMEM((ATT_HEADS, 1), f32),
        ],
    )
    return pl.pallas_call(
        functools.partial(_paged_body, dec_seq=dec_seq),
        grid_spec=grid_spec,
        out_shape=jax.ShapeDtypeStruct((bsz, rows, KV_DIM), f32),
        compiler_params=_cparams(("arbitrary", "arbitrary")),
        name="paged_fox",
    )(page_table, qbd, knt, vnt, lfnt, ckt, cvt, clft)


def _merge_body(ys_ref, ya_ref, gs_ref, ga_ref, h_ref, ws_ref, wa_ref, wo_ref, o_ref):
    a = jax.nn.sigmoid(gs_ref[...]) * _dot(ys_ref[...], ws_ref[...])
    b = jax.nn.sigmoid(ga_ref[...]) * _dot(ya_ref[...], wa_ref[...])
    o_ref[...] = h_ref[...] + _dot((a + b).astype(bf16), wo_ref[...])


def _merge(y_ssd, y_att, main, h, ws, wa, wo, tm):
    n = h.shape[0]
    full = lambda shp: pl.BlockSpec(shp, lambda i: (0, 0))
    return pl.pallas_call(
        _merge_body,
        grid=(n // tm,),
        in_specs=[
            pl.BlockSpec((tm, D_SSM), lambda i: (i, 0)),
            pl.BlockSpec((tm, ATT_DIM), lambda i: (i, 0)),
            pl.BlockSpec((tm, D_MODEL), lambda i: (i, COL_GS // D_MODEL)),
            pl.BlockSpec((tm, D_MODEL), lambda i: (i, COL_GA // D_MODEL)),
            pl.BlockSpec((tm, D_MODEL), lambda i: (i, 0)),
            full((D_SSM, D_MODEL)), full((ATT_DIM, D_MODEL)), full((D_MODEL, D_MODEL)),
        ],
        out_specs=pl.BlockSpec((tm, D_MODEL), lambda i: (i, 0)),
        out_shape=jax.ShapeDtypeStruct((n, D_MODEL), f32),
        compiler_params=_cparams(("parallel",)),
        name="merge",
    )(y_ssd, y_att, main, main, h, ws, wa, wo)


def _pack_w_in(w_in):
    o = np.cumsum([0, D_SSM, CONV_DIM, SSD_HEADS, ATT_DIM, KV_DIM, KV_DIM, ATT_HEADS, D_MODEL, D_MODEL])
    wt = w_in.T
    z, xbc, dt, q, k, v, f, gs, ga = (wt[int(o[i]):int(o[i + 1])] for i in range(9))
    pad = jnp.zeros((COL_XBC - COL_KV - 2 * KV_DIM - SSD_HEADS - ATT_HEADS, D_MODEL), w_in.dtype)
    return jnp.concatenate([z, gs, ga, q, k, v, dt, f, pad, xbc], axis=0).astype(bf16)


def _dense_pre(x2, p, tm):
    h, hn = _ffn(x2, p["ffn1_norm"], p["ffn1_wg"], p["ffn1_wu"], p["ffn1_wd"], tm, p["mix_norm"])
    main = _in_proj(hn, p["w_main"], min(x2.shape[0], 1024), W_MAIN_COLS // 4)
    return h, main


def _dense_post(y_ssd, y_att, main, h, p, tm):
    h2 = _merge(y_ssd, y_att, main, h, p["w_ssd_proj"], p["w_attn_proj"], p["w_out"], tm)
    return _ffn(h2, p["ffn2_norm"], p["ffn2_wg"], p["ffn2_wu"], p["ffn2_wd"], tm)


def _ssd_args(p):
    return (p["conv_w"], p["conv_b"], p["dt_bias"], p["a_log"], p["d_skip_e"], p["ssd_norm"])


def _prompt_layer(x, p):
    bsz, seqlen, _ = x.shape
    n = bsz * seqlen
    tm = 512
    h, main = _dense_pre(x.reshape(n, D_MODEL), p, tm)
    main3 = main.reshape(bsz, seqlen, W_MAIN_COLS)
    qt, ka, vt, k, v, logf, dtp = _qkv_post(main3, p["q_norm_e"], p["k_norm_e"], p["b_f"], 512)
    conv0 = jnp.zeros((bsz, D_CONV - 1, CONV_DIM), f32)
    h0 = jnp.zeros((bsz, SSD_HEADS * SSD_HEAD_DIM, D_STATE), f32)
    y_ssd, h_new, conv_new = _ssd(main3, dtp, conv0, h0, *_ssd_args(p), 4 * SSD_CHUNK, SSD_CHUNK)
    y_att = _flash_fox(qt, ka, vt.reshape(bsz, KV_HEADS, ATT_HEAD_DIM, seqlen), min(seqlen, 2048))
    y = _dense_post(y_ssd.reshape(n, D_SSM), y_att.reshape(n, ATT_DIM), main, h, p, tm)
    return (y.reshape(bsz, seqlen, D_MODEL),
            k.reshape(bsz, seqlen, KV_HEADS, ATT_HEAD_DIM), v.reshape(bsz, seqlen, KV_HEADS, ATT_HEAD_DIM),
            logf, h_new.reshape(bsz, SSD_HEADS, SSD_HEAD_DIM, D_STATE), conv_new)


def _sample_layer(x, conv_prev, h0, ck, cv, clf, page_table, p):
    bsz, t, _ = x.shape
    n = bsz * t
    tm = n
    h, main = _dense_pre(x.reshape(n, D_MODEL), p, tm)
    qt, _, _, k, v, logf, dtp = _qkv_post(main.reshape(1, n, W_MAIN_COLS), p["q_norm_e"], p["k_norm_e"], p["b_f"], n)
    main3 = main.reshape(bsz, t, W_MAIN_COLS)
    y_ssd, h_new, conv_new = _ssd(main3, dtp.reshape(bsz, t, 128), conv_prev,
                                  h0.reshape(bsz, SSD_HEADS * SSD_HEAD_DIM, D_STATE), *_ssd_args(p), t, 16)
    qg = qt[0, :, 0:ATT_HEAD_DIM, :].reshape(KV_HEADS, Q_PER_KV, ATT_HEAD_DIM, bsz, t).transpose(3, 0, 1, 4, 2)
    eye = jnp.eye(KV_HEADS, dtype=bf16)
    qbd = (qg[:, :, :, :, None, :] * eye[None, :, None, None, :, None]).reshape(bsz, ATT_HEADS * t, KV_DIM)
    page = ck.shape[1]
    padt = lambda a: jnp.pad(a.reshape(bsz, t, -1).transpose(0, 2, 1), ((0, 0), (0, 0), (0, page - t)))
    o = _paged_fox(page_table, qbd, padt(k), padt(v), padt(logf),
                   ck.transpose(0, 2, 3, 1), cv.transpose(0, 2, 3, 1), clf.transpose(0, 2, 1), t)
    o6 = o.reshape(bsz, KV_HEADS, Q_PER_KV, t, KV_HEADS, ATT_HEAD_DIM)
    y_att = jnp.stack([o6[:, g, :, :, g, :] for g in range(KV_HEADS)], axis=1)
    y_att = y_att.transpose(0, 3, 1, 2, 4).reshape(n, ATT_DIM).astype(bf16)
    y = _dense_post(y_ssd.reshape(n, D_SSM), y_att, main, h, p, tm)
    return (y.reshape(bsz, t, D_MODEL),
            k.reshape(bsz, t, KV_HEADS, ATT_HEAD_DIM), v.reshape(bsz, t, KV_HEADS, ATT_HEAD_DIM),
            logf.reshape(bsz, t, ATT_HEADS), h_new.reshape(bsz, SSD_HEADS, SSD_HEAD_DIM, D_STATE), conv_new)


def kernel(x_prompt, x_sample, cache_k, cache_v, cache_logf, state_ssm, state_conv, page_table, ffn1_norm, ffn1_w_gate, ffn1_w_up, ffn1_w_down, mix_norm, w_in, conv_w, conv_b, dt_bias, a_log, d_skip, ssd_norm, q_norm, k_norm, b_f, w_ssd_proj, w_attn_proj, w_out, ffn2_norm, ffn2_w_gate, ffn2_w_up, ffn2_w_down):
    depth = w_in.shape[0]
    yp, ys = x_prompt, x_sample
    prompt_states, sample_states = [], []
    for l in range(depth):
        p = {
            "ffn1_norm": ffn1_norm[l][None], "ffn1_wg": ffn1_w_gate[l].astype(bf16),
            "ffn1_wu": ffn1_w_up[l].astype(bf16), "ffn1_wd": ffn1_w_down[l].astype(bf16),
            "mix_norm": mix_norm[l][None], "w_main": _pack_w_in(w_in[l]),
            "conv_w": conv_w[l], "conv_b": conv_b[l][None], "dt_bias": dt_bias[l][None], "a_log": a_log[l][None],
            "d_skip_e": jnp.repeat(d_skip[l], SSD_HEAD_DIM)[None], "ssd_norm": ssd_norm[l][None],
            "q_norm_e": (jnp.tile(q_norm[l], ATT_HEADS) * (ATT_HEAD_DIM ** -0.5))[None], "k_norm_e": jnp.tile(k_norm[l], KV_HEADS)[None],
            "b_f": b_f[l][None],
            "w_ssd_proj": w_ssd_proj[l].astype(bf16), "w_attn_proj": w_attn_proj[l].astype(bf16),
            "w_out": w_out[l].astype(bf16),
            "ffn2_norm": ffn2_norm[l][None], "ffn2_wg": ffn2_w_gate[l].astype(bf16),
            "ffn2_wu": ffn2_w_up[l].astype(bf16), "ffn2_wd": ffn2_w_down[l].astype(bf16),
        }
        yp, *sp = _prompt_layer(yp, p)
        ys, *ss = _sample_layer(ys, state_conv[l], state_ssm[l], cache_k[l], cache_v[l], cache_logf[l],
                                page_table, p)
        prompt_states.append(sp)
        sample_states.append(ss)
    k_p, v_p, logf_p, ssm_p, conv_p = (jnp.stack(t) for t in zip(*prompt_states))
    k_s, v_s, logf_s, ssm_s, conv_s = (jnp.stack(t) for t in zip(*sample_states))
    return (yp, ys, k_p, v_p, logf_p, ssm_p, conv_p, k_s, v_s, logf_s, ssm_s, conv_s)
```
